```python
import jax, jax.numpy as jnp
from jax import lax
import numpy as np

D_MODEL = 1024
BATCH = 2
SEQ = 8192
DEPTH = 2

N_MIXERS = 2
N_FOX = (DEPTH + 1) // 2
N_HGRN = DEPTH // 2
FOX_HEADS = 16
FOX_HEAD_DIM = D_MODEL // FOX_HEADS
FOX_BLOCK_Q = 128
HGRN_EXPAND = 128
HGRN_HEADS = D_MODEL // HGRN_EXPAND
HGRN_CHUNK = 64
D_FF = ((8 * D_MODEL // 3 + 127) // 128) * 128
N_SUB = 3
EPS = 1e-6

kernel_name = "fox_hgrn2_macaron_adaln_hybrid"


def _normal(k, shape, scale):
    return jax.random.normal(k, shape, jnp.float32) * scale


def setup_inputs(seed: int = 0) -> dict:
    key = jax.random.key(seed)
    ks = jax.random.split(key, 16)
    D, F = D_MODEL, D_FF
    return {
        "x": _normal(ks[0], (BATCH, SEQ, D), 1.0),
        "c": _normal(ks[1], (BATCH, D), 1.0),
        "ada_w": _normal(ks[2], (DEPTH, D, N_SUB * 3 * D), D ** -0.5),
        "ada_b": _normal(ks[3], (DEPTH, N_SUB * 3 * D), 0.02),
        "norm_g": 1.0 + _normal(ks[4], (DEPTH, N_SUB, D), 0.02),
        "ffn_w_up": _normal(ks[5], (DEPTH, 2, D, 2 * F), D ** -0.5),
        "ffn_w_down": _normal(ks[6], (DEPTH, 2, F, D), F ** -0.5),
        "fox_w_in": _normal(ks[7], (N_FOX, D, 4 * D + FOX_HEADS), D ** -0.5),
        "fox_b_f": 2.0 + _normal(ks[8], (N_FOX, FOX_HEADS), 0.5),
        "fox_w_out": _normal(ks[9], (N_FOX, D, D), D ** -0.5),
        "hgrn_w_in": _normal(ks[10], (N_HGRN, D, 4 * D), D ** -0.5),
        "hgrn_norm_g": 1.0 + _normal(ks[11], (N_HGRN, D), 0.02),
        "hgrn_w_out": _normal(ks[12], (N_HGRN, D, D), D ** -0.5),
        "hgrn_lb_logits": _normal(ks[13], (DEPTH, D), 0.5),
        "final_norm_g": 1.0 + _normal(ks[14], (D,), 0.02),
    }


def rms_norm(x, g):
    xf = x.astype(jnp.float32)
    y = xf * lax.rsqrt(jnp.mean(xf * xf, axis=-1, keepdims=True) + EPS)
    return (y * g.astype(jnp.float32)).astype(x.dtype)


def swiglu(h, w_up, w_down):
    a, b = jnp.split(h @ w_up, 2, axis=-1)
    return (jax.nn.silu(a) * b) @ w_down


def fox_attention(h, w_in, b_f, w_out):
    B, S, D = h.shape
    H, Dh, BQ = FOX_HEADS, FOX_HEAD_DIM, FOX_BLOCK_Q
    proj = h @ w_in
    heads = lambda t: t.reshape(B, S, H, Dh).transpose(0, 2, 1, 3)
    q = heads(proj[..., 0:D])
    k = heads(proj[..., D:2 * D])
    v = heads(proj[..., 2 * D:3 * D])
    g = proj[..., 3 * D:4 * D]
    logf = jax.nn.log_sigmoid((proj[..., 4 * D:] + b_f).astype(jnp.float32))
    cum = jnp.cumsum(logf, axis=1).transpose(0, 2, 1)
    nb = S // BQ
    q_blocks = q.reshape(B, H, nb, BQ, Dh).transpose(2, 0, 1, 3, 4)
    c_blocks = cum.reshape(B, H, nb, BQ).transpose(2, 0, 1, 3)
    k_pos = jnp.arange(S)
    scale = Dh ** -0.5

    def block(args):
        qb, cb, start = args
        s = (jnp.einsum('bhqd,bhkd->bhqk', qb, k).astype(jnp.float32) * scale
             + cb[..., None] - cum[:, :, None, :])
        q_pos = start + jnp.arange(BQ)
        s = jnp.where(k_pos[None, :] <= q_pos[:, None], s, -jnp.inf)
        p = jax.nn.softmax(s, axis=-1)
        return jnp.einsum('bhqk,bhkd->bhqd', p.astype(v.dtype), v)

    o = lax.map(block, (q_blocks, c_blocks, jnp.arange(nb) * BQ))
    o = o.transpose(1, 0, 3, 2, 4).reshape(B, S, D)
    o = o * jax.nn.sigmoid(g)
    return o @ w_out


def hgrn2(h, w_in, lb, g_norm, w_out):
    B, S, D = h.shape
    H, K, C = HGRN_HEADS, HGRN_EXPAND, HGRN_CHUNK
    nc = S // C
    proj = (h @ w_in).astype(jnp.float32)
    q = proj[..., 0:D]
    f = lb + (1.0 - lb) * jax.nn.sigmoid(proj[..., D:2 * D])
    v = jax.nn.silu(proj[..., 2 * D:3 * D])
    g = proj[..., 3 * D:4 * D]
    logf = jnp.log(f)
    k = 1.0 - f
    to_chunks = lambda t: t.reshape(B, nc, C, H, K).transpose(1, 0, 3, 2, 4)
    tri = jnp.tril(jnp.ones((C, C), dtype=bool))

    def step(state, inp):
        qc, kc, vc, gc = inp
        G = jnp.cumsum(gc, axis=2)
        o_inter = jnp.einsum('bhtk,bhkv->bhtv', qc * jnp.exp(G), state)
        diff = G[:, :, :, None, :] - G[:, :, None, :, :]
        decay = jnp.exp(jnp.where(tri[:, :, None], diff, -jnp.inf))
        A = jnp.einsum('bhtk,bhsk,bhtsk->bhts', qc, kc, decay)
        o_intra = jnp.einsum('bhts,bhsv->bhtv', A, vc)
        G_last = G[:, :, -1, :]
        state = (jnp.exp(G_last)[..., None] * state
                 + jnp.einsum('bhsk,bhsv->bhkv', kc * jnp.exp(G_last[:, :, None, :] - G), vc))
        return state, o_inter + o_intra

    state0 = jnp.zeros((B, H, K, K), jnp.float32)
    _, o = lax.scan(step, state0, (to_chunks(q), to_chunks(k), to_chunks(v), to_chunks(logf)))
    o = o.transpose(1, 0, 3, 2, 4)
    o = o * lax.rsqrt(jnp.mean(o * o, axis=-1, keepdims=True) + EPS)
    o = o * g_norm.astype(jnp.float32).reshape(H, K)
    o = o.reshape(B, S, D) * jax.nn.silu(g)
    return o.astype(h.dtype) @ w_out


def reference(x, c, ada_w, ada_b, norm_g, ffn_w_up, ffn_w_down, fox_w_in, fox_b_f,
              fox_w_out, hgrn_w_in, hgrn_norm_g, hgrn_w_out, hgrn_lb_logits, final_norm_g):
    B, S, D = x.shape
    cond = jax.nn.silu(c)
    sm = jax.nn.softmax(hgrn_lb_logits.astype(jnp.float32), axis=0)
    lower_bounds = jnp.cumsum(sm, axis=0) - sm[0]
    for i in range(DEPTH):
        mod = (cond @ ada_w[i] + ada_b[i]).reshape(B, N_SUB, 3, 1, D)
        h = rms_norm(x, norm_g[i, 0]) * (1.0 + mod[:, 0, 1]) + mod[:, 0, 0]
        x = x + 0.5 * mod[:, 0, 2] * swiglu(h, ffn_w_up[i, 0], ffn_w_down[i, 0])
        h = rms_norm(x, norm_g[i, 1]) * (1.0 + mod[:, 1, 1]) + mod[:, 1, 0]
        j = i // N_MIXERS
        if i % N_MIXERS == 0:
            y = fox_attention(h, fox_w_in[j], fox_b_f[j], fox_w_out[j])
        else:
            y = hgrn2(h, hgrn_w_in[j], lower_bounds[i], hgrn_norm_g[j], hgrn_w_out[j])
        x = x + mod[:, 1, 2] * y
        h = rms_norm(x, norm_g[i, 2]) * (1.0 + mod[:, 2, 1]) + mod[:, 2, 0]
        x = x + 0.5 * mod[:, 2, 2] * swiglu(h, ffn_w_up[i, 1], ffn_w_down[i, 1])
    return rms_norm(x, final_norm_g)
```

```python
import functools

import jax
import jax.numpy as jnp
import numpy as np
from jax import lax
from jax.experimental import pallas as pl
from jax.experimental.pallas import tpu as pltpu

N_SUB = 3
FOX_HEADS = 16
FOX_HEAD_DIM = 64
HGRN_EXPAND = 128
EPS = 1e-6

LANES = 128
SUBLANES = 8
VMEM_LIMIT_BYTES = 56 * 1024 * 1024

ROW_TILE = 1024
FFN_CHUNK = 256
ATT_BLOCK = 512
HGRN_CHUNK = 128
HGRN_HEADS_PER_STEP = 2
N_SPLIT = 3

BF16 = jnp.bfloat16
F32 = jnp.float32


def _params(*sem):
    return pltpu.CompilerParams(dimension_semantics=sem, vmem_limit_bytes=VMEM_LIMIT_BYTES)


def _resident(shape):
    zeros = (0,) * len(shape)
    return pl.BlockSpec(shape, lambda *_: zeros, pipeline_mode=pl.Buffered(1))


def _row_spec(tm, width):
    return pl.BlockSpec((1, tm, width), lambda b, i: (b, i, 0))


def _batch_vec_spec(width):
    return pl.BlockSpec((1, 1, width), lambda b, i: (b, 0, 0))


def _dot(a, b):
    return jnp.dot(a, b, preferred_element_type=F32)


def _dot_nt(a, b):
    return lax.dot_general(a, b, (((1,), (1,)), ((), ())), preferred_element_type=F32)


def _dot_tn(a, b):
    return lax.dot_general(a, b, (((0,), (0,)), ((), ())), preferred_element_type=F32)


def _sigmoid(x):
    return 1.0 / (1.0 + jnp.exp(-x))


def _silu(x):
    return x * _sigmoid(x)


def _modulated_norm(x, norm_g, scale, shift):
    y = x * lax.rsqrt(jnp.mean(x * x, axis=-1, keepdims=True) + EPS)
    return (y * norm_g) * (1.0 + scale) + shift


def _split_bf16(x, n):
    pieces = []
    rest = x
    for _ in range(n):
        p = rest.astype(BF16)
        pieces.append(p)
        rest = rest - p.astype(F32)
    return pieces


def _ada_kernel(c_ref, w_ref, b_ref, o_ref):
    cond = _silu(c_ref[...])
    y = jnp.dot(cond, w_ref[0], preferred_element_type=F32,
                precision=lax.Precision.HIGHEST)
    o_ref[0] = y + b_ref[0]


def _ada_modulation(c, ada_w, ada_b):
    depth, d, n = ada_w.shape
    b = c.shape[0]
    bp = -(-b // SUBLANES) * SUBLANES
    tn = n // 9 if n % (9 * LANES) == 0 else n
    c_pad = jnp.zeros((bp, d), F32).at[:b].set(c)
    out = pl.pallas_call(
        _ada_kernel,
        grid=(depth, n // tn),
        in_specs=[
            pl.BlockSpec((bp, d), lambda i, j: (0, 0)),
            pl.BlockSpec((1, d, tn), lambda i, j: (i, 0, j)),
            pl.BlockSpec((1, 1, tn), lambda i, j: (i, 0, j)),
        ],
        out_specs=pl.BlockSpec((1, bp, tn), lambda i, j: (i, 0, j)),
        out_shape=jax.ShapeDtypeStruct((depth, bp, n), F32),
        compiler_params=_params("parallel", "parallel"),
        name="ada_modulation",
    )(c_pad, ada_w, ada_b.reshape(depth, 1, n))
    return out[:, :b]


def _ffn_kernel(x_ref, shift_ref, scale_ref, gate_ref, ng_ref, wup_ref, wdn_ref, fg_ref,
                o_ref, h_ref, acc_ref, *, n_chunks, tf, final_norm):
    x = x_ref[0]
    h_ref[...] = _modulated_norm(x, ng_ref[...], scale_ref[0], shift_ref[0]).astype(BF16)
    acc_ref[...] = jnp.zeros_like(acc_ref)

    def body(f, carry):
        ab = _dot(h_ref[...], wup_ref[f])
        g = _silu(ab[:, :tf]) * ab[:, tf:]
        acc_ref[...] += _dot(g.astype(BF16), wdn_ref[f])
        return carry

    lax.fori_loop(0, n_chunks, body, 0)
    y = x + (0.5 * gate_ref[0]) * acc_ref[...]
    if final_norm:
        y = (y * lax.rsqrt(jnp.mean(y * y, axis=-1, keepdims=True) + EPS)) * fg_ref[...]
    o_ref[0] = y


def _ffn(x, shift, scale, gate, norm_g, w_up, w_down, final_g, final_norm):
    b, s, d = x.shape
    f = w_down.shape[0]
    tf = FFN_CHUNK
    n_chunks = f // tf
    tm = ROW_TILE
    wa = w_up[:, :f].reshape(d, n_chunks, tf)
    wb = w_up[:, f:].reshape(d, n_chunks, tf)
    wup = jnp.concatenate([wa, wb], axis=-1).transpose(1, 0, 2).astype(BF16)
    wdn = w_down.reshape(n_chunks, tf, d).astype(BF16)
    kern = functools.partial(_ffn_kernel, n_chunks=n_chunks, tf=tf, final_norm=final_norm)
    return pl.pallas_call(
        kern,
        grid=(b, s // tm),
        in_specs=[
            _row_spec(tm, d), _batch_vec_spec(d), _batch_vec_spec(d), _batch_vec_spec(d),
            _resident((1, d)), _resident((n_chunks, d, 2 * tf)), _resident((n_chunks, tf, d)),
            _resident((1, d)),
        ],
        out_specs=_row_spec(tm, d),
        out_shape=jax.ShapeDtypeStruct((b, s, d), F32),
        scratch_shapes=[pltpu.VMEM((tm, d), BF16), pltpu.VMEM((tm, d), F32)],
        compiler_params=_params("parallel", "parallel"),
        name="ffn_final" if final_norm else "ffn",
    )(x, shift, scale, gate, norm_g.reshape(1, d), wup, wdn, final_g.reshape(1, d))


def _out_proj_kernel(x_ref, o_ref, gate_ref, w_ref, y_ref):
    y_ref[0] = x_ref[0] + gate_ref[0] * _dot(o_ref[0], w_ref[...])


def _out_proj(x, o, gate, w_out):
    b, s, d = x.shape
    tm = ROW_TILE
    return pl.pallas_call(
        _out_proj_kernel,
        grid=(b, s // tm),
        in_specs=[_row_spec(tm, d), _row_spec(tm, d), _batch_vec_spec(d), _resident((d, d))],
        out_specs=_row_spec(tm, d),
        out_shape=jax.ShapeDtypeStruct((b, s, d), F32),
        compiler_params=_params("parallel", "parallel"),
        name="out_proj",
    )(x, o, gate, w_out.astype(BF16))


FOX_SLOT = 2 * FOX_HEAD_DIM
ONES_LANE = N_SPLIT * FOX_HEADS


def _fox_proj_kernel(x_ref, shift_ref, scale_ref, ng_ref, wqk_ref, wvg_ref, wf_ref, bf_ref,
                     tri_ref, place_ref, qa_ref, ka_ref, v_ref, sg_ref, carry_ref, *, d):
    @pl.when(pl.program_id(1) == 0)
    def _():
        carry_ref[...] = jnp.zeros_like(carry_ref)

    h = _modulated_norm(x_ref[0], ng_ref[...], scale_ref[0], shift_ref[0]).astype(BF16)
    z = _dot(h, wf_ref[...]) + bf_ref[...]
    logf = jnp.minimum(z, 0.0) - jnp.log(1.0 + jnp.exp(-jnp.abs(z)))
    pieces = jnp.concatenate(_split_bf16(logf, N_SPLIT), axis=1)
    part = _dot(tri_ref[...], pieces)
    cum = carry_ref[...] + ((part[:, :LANES] + part[:, LANES:2 * LANES]) + part[:, 2 * LANES:])
    carry_ref[...] = cum[-1:, :]

    lane = lax.broadcasted_iota(jnp.int32, cum.shape, 1)
    operand = jnp.where(lane == ONES_LANE, 1.0, 0.0).astype(F32)
    for j, p in enumerate(_split_bf16(cum, N_SPLIT)):
        pf = p.astype(F32)
        shifted = pf if j == 0 else pltpu.roll(pf, j * FOX_HEADS, axis=1)
        operand = jnp.where((lane >= j * FOX_HEADS) & (lane < (j + 1) * FOX_HEADS), shifted, operand)
    operand = operand.astype(BF16)
    width = qa_ref.shape[2]
    qa_ref[0] = (_dot(h, wqk_ref[:, :width]) + _dot(operand, place_ref[:, :width])).astype(BF16)
    ka_ref[0] = (_dot(h, wqk_ref[:, width:]) + _dot(operand, place_ref[:, width:])).astype(BF16)
    v_ref[0] = _dot(h, wvg_ref[:, :d]).astype(BF16)
    sg_ref[0] = _sigmoid(_dot(h, wvg_ref[:, d:])).astype(BF16)


def _fox_placement():
    place = np.zeros((LANES, 2 * FOX_HEADS * FOX_SLOT), np.float32)
    k_off = FOX_HEADS * FOX_SLOT
    for h in range(FOX_HEADS):
        base = h * FOX_SLOT + FOX_HEAD_DIM
        for j in range(N_SPLIT):
            place[j * FOX_HEADS + h, base + j] = 1.0
            place[ONES_LANE, base + N_SPLIT + j] = 1.0
            place[ONES_LANE, k_off + base + j] = 1.0
            place[j * FOX_HEADS + h, k_off + base + N_SPLIT + j] = -1.0
    return jnp.asarray(place, BF16)


def _fox_proj(x, shift, scale, norm_g, w_in, b_f):
    b, s, d = x.shape
    hds, dh = FOX_HEADS, FOX_HEAD_DIM
    tm = ROW_TILE // 2
    att_scale = dh ** -0.5

    def slots(w):
        w = w.reshape(d, hds, dh)
        return jnp.concatenate([w, jnp.zeros_like(w)], axis=-1).reshape(d, hds * FOX_SLOT)

    wqk = jnp.concatenate([slots(w_in[:, 0:d] * att_scale), slots(w_in[:, d:2 * d])], axis=1).astype(BF16)
    wvg = w_in[:, 2 * d:4 * d].astype(BF16)
    wf = jnp.zeros((d, LANES), F32).at[:, :hds].set(w_in[:, 4 * d:]).astype(BF16)
    bf = jnp.zeros((1, LANES), F32).at[0, :hds].set(b_f)
    tri = jnp.tril(jnp.ones((tm, tm), F32)).astype(BF16)
    qw = hds * FOX_SLOT
    kern = functools.partial(_fox_proj_kernel, d=d)
    return pl.pallas_call(
        kern,
        grid=(b, s // tm),
        in_specs=[
            _row_spec(tm, d), _batch_vec_spec(d), _batch_vec_spec(d), _resident((1, d)),
            _resident((d, 2 * qw)), _resident((d, 2 * d)), _resident((d, LANES)),
            _resident((1, LANES)), _resident((tm, tm)), _resident((LANES, 2 * qw)),
        ],
        out_specs=[_row_spec(tm, qw), _row_spec(tm, qw), _row_spec(tm, d), _row_spec(tm, d)],
        out_shape=[
            jax.ShapeDtypeStruct((b, s, qw), BF16), jax.ShapeDtypeStruct((b, s, qw), BF16),
            jax.ShapeDtypeStruct((b, s, d), BF16), jax.ShapeDtypeStruct((b, s, d), BF16),
        ],
        scratch_shapes=[pltpu.VMEM((1, LANES), F32)],
        compiler_params=_params("parallel", "arbitrary"),
        name="fox_proj",
    )(x, shift, scale, norm_g.reshape(1, d), wqk, wvg, wf, bf, tri, _fox_placement())


def _fox_attn_kernel(qa_ref, ka_ref, v_ref, sg_ref, o_ref, m_ref, l_ref, acc_ref, *, blk):
    i = pl.program_id(2)
    dh = FOX_HEAD_DIM
    m_ref[...] = jnp.full_like(m_ref, -jnp.inf)
    l_ref[...] = jnp.zeros_like(l_ref)
    acc_ref[...] = jnp.zeros_like(acc_ref)
    q = qa_ref[0]
    first = lax.broadcasted_iota(jnp.int32, (blk, 2 * dh), 1) < dh

    def step(j, masked):
        start = pl.multiple_of(j * blk, blk)
        k = ka_ref[0, pl.ds(start, blk), :]
        v = v_ref[0, pl.ds(start, blk), :]
        pv, alpha = [], []
        for hh in range(2):
            s = _dot_nt(q[:, hh * FOX_SLOT:(hh + 1) * FOX_SLOT], k[:, hh * FOX_SLOT:(hh + 1) * FOX_SLOT])
            if masked:
                qp = lax.broadcasted_iota(jnp.int32, s.shape, 0)
                kp = lax.broadcasted_iota(jnp.int32, s.shape, 1)
                s = jnp.where(kp <= qp, s, -jnp.inf)
            m_old = m_ref[hh]
            m_new = jnp.maximum(m_old, jnp.max(s, axis=-1, keepdims=True))
            a = jnp.exp(m_old - m_new)
            p = jnp.exp(s - m_new)
            l_ref[hh] = a * l_ref[hh] + jnp.sum(p, axis=-1, keepdims=True)
            m_ref[hh] = m_new
            pv.append(_dot(p.astype(BF16), v))
            alpha.append(a)
        acc_ref[...] = jnp.where(first, alpha[0], alpha[1]) * acc_ref[...] + jnp.where(first, pv[0], pv[1])

    def body(j, carry):
        step(j, False)
        return carry

    lax.fori_loop(0, i, body, 0)
    step(i, True)
    inv = jnp.where(first, 1.0 / l_ref[0], 1.0 / l_ref[1])
    o_ref[0] = (acc_ref[...] * inv * sg_ref[0].astype(F32)).astype(BF16)


def _fox_attention(qa, ka, v, sg):
    b, s, d = v.shape
    blk = ATT_BLOCK
    pairs = FOX_HEADS // 2
    kern = functools.partial(_fox_attn_kernel, blk=blk)
    return pl.pallas_call(
        kern,
        grid=(b, pairs, s // blk),
        in_specs=[
            pl.BlockSpec((1, blk, 2 * FOX_SLOT), lambda bb, p, i: (bb, i, p)),
            pl.BlockSpec((1, s, 2 * FOX_SLOT), lambda bb, p, i: (bb, 0, p)),
            pl.BlockSpec((1, s, 2 * FOX_HEAD_DIM), lambda bb, p, i: (bb, 0, p)),
            pl.BlockSpec((1, blk, 2 * FOX_HEAD_DIM), lambda bb, p, i: (bb, i, p)),
        ],
        out_specs=pl.BlockSpec((1, blk, 2 * FOX_HEAD_DIM), lambda bb, p, i: (bb, i, p)),
        out_shape=jax.ShapeDtypeStruct((b, s, d), BF16),
        scratch_shapes=[
            pltpu.VMEM((2, blk, 1), F32), pltpu.VMEM((2, blk, 1), F32),
            pltpu.VMEM((blk, 2 * FOX_HEAD_DIM), F32),
        ],
        compiler_params=_params("parallel", "parallel", "arbitrary"),
        name="fox_attention",
    )(qa, ka, v, sg)


def _hgrn_proj_kernel(x_ref, shift_ref, scale_ref, ng_ref, w_ref, lbl_ref,
                      q_ref, kk_ref, logf_ref, v_ref, sg_ref, *, d, layer):
    lg = lbl_ref[...]
    e = jnp.exp(lg - jnp.max(lg, axis=0, keepdims=True))
    sm = e / jnp.sum(e, axis=0, keepdims=True)
    lb = jnp.sum(sm[0:layer + 1], axis=0, keepdims=True) - sm[0:1]

    h = _modulated_norm(x_ref[0], ng_ref[...], scale_ref[0], shift_ref[0]).astype(BF16)
    proj = _dot(h, w_ref[...])
    f = lb + (1.0 - lb) * _sigmoid(proj[:, d:2 * d])
    q_ref[0] = proj[:, 0:d].astype(BF16)
    kk_ref[0] = (1.0 - f).astype(BF16)
    logf_ref[0] = jnp.log(f)
    v_ref[0] = _silu(proj[:, 2 * d:3 * d]).astype(BF16)
    sg_ref[0] = _silu(proj[:, 3 * d:4 * d]).astype(BF16)


def _hgrn_proj(x, shift, scale, norm_g, w_in, lb_logits, layer):
    b, s, d = x.shape
    depth = lb_logits.shape[0]
    tm = ROW_TILE // 2
    kern = functools.partial(_hgrn_proj_kernel, d=d, layer=layer)
    bf = jax.ShapeDtypeStruct((b, s, d), BF16)
    return pl.pallas_call(
        kern,
        grid=(b, s // tm),
        in_specs=[
            _row_spec(tm, d), _batch_vec_spec(d), _batch_vec_spec(d), _resident((1, d)),
            _resident((d, 4 * d)), _resident((depth, d)),
        ],
        out_specs=[_row_spec(tm, d)] * 5,
        out_shape=[bf, bf, jax.ShapeDtypeStruct((b, s, d), F32), bf, bf],
        compiler_params=_params("parallel", "parallel"),
        name="hgrn_proj",
    )(x, shift, scale, norm_g.reshape(1, d), w_in.astype(BF16), lb_logits.astype(F32))


def _prefix_rows(x):
    n = x.shape[0]
    row = lax.broadcasted_iota(jnp.int32, x.shape, 0)
    shift = 1
    while shift < n:
        x = x + jnp.where(row >= shift, pltpu.roll(x, shift, axis=0), 0.0)
        shift *= 2
    return x


def _pivot_rows(g, half):
    n, lanes = g.shape
    if 2 * half >= SUBLANES:
        g3 = g.reshape(n // (2 * half), 2 * half, lanes)
        piv = jnp.broadcast_to(g3[:, half - 1:half, :], g3.shape)
        return piv.reshape(n, lanes)
    g3 = g.reshape(n // SUBLANES, SUBLANES, lanes)
    sub = lax.broadcasted_iota(jnp.int32, g3.shape, 1)
    out = None
    for grp in range(SUBLANES // (2 * half)):
        r = grp * 2 * half + half - 1
        cand = jnp.broadcast_to(g3[:, r:r + 1, :], g3.shape)
        out = cand if out is None else jnp.where(sub >= grp * 2 * half, cand, out)
    return out.reshape(n, lanes)


def _hgrn_chunk_kernel(q_ref, kk_ref, logf_ref, v_ref, sg_ref, gn_ref, o_ref, state_ref,
                       *, chunk, heads):
    @pl.when(pl.program_id(2) == 0)
    def _():
        state_ref[...] = jnp.zeros_like(state_ref)

    kdim = HGRN_EXPAND
    t_idx = lax.broadcasted_iota(jnp.int32, (chunk, chunk), 0)
    s_idx = lax.broadcasted_iota(jnp.int32, (chunk, chunk), 1)
    row = lax.broadcasted_iota(jnp.int32, (chunk, kdim), 0)
    for hh in range(heads):
        cols = slice(hh * kdim, (hh + 1) * kdim)
        q = q_ref[0, :, cols].astype(F32)
        kk = kk_ref[0, :, cols].astype(F32)
        v = v_ref[0, :, cols]
        g = _prefix_rows(logf_ref[0, :, cols])

        a = jnp.where(t_idx == s_idx, jnp.sum(q * kk, axis=-1, keepdims=True), 0.0)
        half = chunk // 2
        while half >= 1:
            e = jnp.exp(-jnp.abs(g - _pivot_rows(g, half)))
            upper = (row & half) != 0
            qe = jnp.where(upper, q * e, 0.0).astype(BF16)
            ke = jnp.where(upper, 0.0, kk * e).astype(BF16)
            same_block = (t_idx ^ s_idx) < 2 * half
            a = a + jnp.where(same_block, _dot_nt(qe, ke), 0.0)
            half //= 2

        g_last = g[chunk - 1:chunk, :]
        state_t = state_ref[hh]
        o = _dot_nt((q * jnp.exp(g)).astype(BF16), state_t.astype(BF16)) + _dot(a.astype(BF16), v)
        k_tail = (kk * jnp.exp(g_last - g)).astype(BF16)
        state_ref[hh] = state_t * jnp.exp(g_last) + _dot_tn(v, k_tail)

        o = o * lax.rsqrt(jnp.mean(o * o, axis=-1, keepdims=True) + EPS)
        o = (o * gn_ref[:, cols]) * sg_ref[0, :, cols].astype(F32)
        o_ref[0, :, cols] = o.astype(BF16)


def _hgrn_chunks(q, kk, logf, v, sg, g_norm):
    b, s, d = q.shape
    chunk, heads = HGRN_CHUNK, HGRN_HEADS_PER_STEP
    width = heads * HGRN_EXPAND
    spec = pl.BlockSpec((1, chunk, width), lambda bb, hp, c: (bb, c, hp))
    kern = functools.partial(_hgrn_chunk_kernel, chunk=chunk, heads=heads)
    return pl.pallas_call(
        kern,
        grid=(b, d // width, s // chunk),
        in_specs=[spec] * 5 + [pl.BlockSpec((1, width), lambda bb, hp, c: (0, hp))],
        out_specs=spec,
        out_shape=jax.ShapeDtypeStruct((b, s, d), BF16),
        scratch_shapes=[pltpu.VMEM((heads, HGRN_EXPAND, HGRN_EXPAND), F32)],
        compiler_params=_params("parallel", "parallel", "arbitrary"),
        name="hgrn_chunks",
    )(q, kk, logf, v, sg, g_norm.reshape(1, d).astype(F32))


def kernel(x, c, ada_w, ada_b, norm_g, ffn_w_up, ffn_w_down, fox_w_in, fox_b_f, fox_w_out,
           hgrn_w_in, hgrn_norm_g, hgrn_w_out, hgrn_lb_logits, final_norm_g):
    b, s, d = x.shape
    depth = ada_w.shape[0]
    mod = _ada_modulation(c, ada_w, ada_b).reshape(depth, b, N_SUB, 3, 1, d)
    for i in range(depth):
        shift = lambda sub: mod[i, :, sub, 0]
        scale = lambda sub: mod[i, :, sub, 1]
        gate = lambda sub: mod[i, :, sub, 2]
        x = _ffn(x, shift(0), scale(0), gate(0), norm_g[i, 0], ffn_w_up[i, 0], ffn_w_down[i, 0],
                 final_norm_g, False)
        j = i // 2
        if i % 2 == 0:
            qa, ka, v, sg = _fox_proj(x, shift(1), scale(1), norm_g[i, 1], fox_w_in[j], fox_b_f[j])
            o = _fox_attention(qa, ka, v, sg)
            x = _out_proj(x, o, gate(1), fox_w_out[j])
        else:
            q, kk, logf, v, sg = _hgrn_proj(x, shift(1), scale(1), norm_g[i, 1], hgrn_w_in[j],
                                            hgrn_lb_logits, i)
            o = _hgrn_chunks(q, kk, logf, v, sg, hgrn_norm_g[j])
            x = _out_proj(x, o, gate(1), hgrn_w_out[j])
        x = _ffn(x, shift(2), scale(2), gate(2), norm_g[i, 2], ffn_w_up[i, 1], ffn_w_down[i, 1],
                 final_norm_g, i == depth - 1)
    return x
```

```python
import functools

import jax
import jax.numpy as jnp
import numpy as np
from jax import lax
from jax.experimental import pallas as pl
from jax.experimental.pallas import tpu as pltpu

N_SUB = 3
FOX_HEADS = 16
FOX_HEAD_DIM = 64
HGRN_EXPAND = 128
EPS = 1e-6

LANES = 128
SUBLANES = 8
VMEM_LIMIT_BYTES = 56 * 1024 * 1024

ROW_TILE = 1024
FFN_CHUNK = 256
ATT_BLOCK = 512
HGRN_CHUNK = 128
HGRN_HEADS_PER_STEP = 2
N_SPLIT = 3

BF16 = jnp.bfloat16
F32 = jnp.float32


def _params(*sem):
    return pltpu.CompilerParams(dimension_semantics=sem, vmem_limit_bytes=VMEM_LIMIT_BYTES)


def _resident(shape):
    zeros = (0,) * len(shape)
    return pl.BlockSpec(shape, lambda *_: zeros, pipeline_mode=pl.Buffered(1))


def _row_spec(tm, width):
    return pl.BlockSpec((1, tm, width), lambda b, i: (b, i, 0))


def _block_major_spec(features, tm):
    return pl.BlockSpec((1, 1, features, tm), lambda b, i: (b, i, 0, 0))


def _batch_vec_spec(width):
    return pl.BlockSpec((1, 1, width), lambda b, i: (b, 0, 0))


def _dot(a, b):
    return jnp.dot(a, b, preferred_element_type=F32)


def _dot_nt(a, b):
    return lax.dot_general(a, b, (((1,), (1,)), ((), ())), preferred_element_type=F32)


def _dot_tn(a, b):
    return lax.dot_general(a, b, (((0,), (0,)), ((), ())), preferred_element_type=F32)


def _sigmoid(x):
    return 1.0 / (1.0 + jnp.exp(-x))


def _silu(x):
    return x * _sigmoid(x)


def _modulated_norm(x, norm_g, scale, shift):
    y = x * lax.rsqrt(jnp.mean(x * x, axis=-1, keepdims=True) + EPS)
    return (y * norm_g) * (1.0 + scale) + shift


def _split_bf16(x, n):
    pieces = []
    rest = x
    for _ in range(n):
        p = rest.astype(BF16)
        pieces.append(p)
        rest = rest - p.astype(F32)
    return pieces


def _ada_kernel(c_ref, w_ref, b_ref, o_ref):
    cond = _silu(c_ref[...])
    y = jnp.dot(cond, w_ref[0], preferred_element_type=F32,
                precision=lax.Precision.HIGHEST)
    o_ref[0] = y + b_ref[0]


def _ada_modulation(c, ada_w, ada_b):
    depth, d, n = ada_w.shape
    b = c.shape[0]
    bp = -(-b // SUBLANES) * SUBLANES
    tn = n // 9 if n % (9 * LANES) == 0 else n
    c_pad = jnp.zeros((bp, d), F32).at[:b].set(c)
    out = pl.pallas_call(
        _ada_kernel,
        grid=(depth, n // tn),
        in_specs=[
            pl.BlockSpec((bp, d), lambda i, j: (0, 0)),
            pl.BlockSpec((1, d, tn), lambda i, j: (i, 0, j)),
            pl.BlockSpec((1, 1, tn), lambda i, j: (i, 0, j)),
        ],
        out_specs=pl.BlockSpec((1, bp, tn), lambda i, j: (i, 0, j)),
        out_shape=jax.ShapeDtypeStruct((depth, bp, n), F32),
        compiler_params=_params("parallel", "parallel"),
        name="ada_modulation",
    )(c_pad, ada_w, ada_b.reshape(depth, 1, n))
    return out[:, :b]


def _ffn_kernel(x_ref, shift_ref, scale_ref, gate_ref, ng_ref, wup_ref, wdn_ref, fg_ref,
                o_ref, h_ref, acc_ref, *, n_chunks, tf, final_norm):
    x = x_ref[0]
    h_ref[...] = _modulated_norm(x, ng_ref[...], scale_ref[0], shift_ref[0]).astype(BF16)
    acc_ref[...] = jnp.zeros_like(acc_ref)

    def body(f, carry):
        ab = _dot(h_ref[...], wup_ref[f])
        g = _silu(ab[:, :tf]) * ab[:, tf:]
        acc_ref[...] += _dot(g.astype(BF16), wdn_ref[f])
        return carry

    lax.fori_loop(0, n_chunks, body, 0)
    y = x + (0.5 * gate_ref[0]) * acc_ref[...]
    if final_norm:
        y = (y * lax.rsqrt(jnp.mean(y * y, axis=-1, keepdims=True) + EPS)) * fg_ref[...]
    o_ref[0] = y


def _ffn(x, shift, scale, gate, norm_g, w_up, w_down, final_g, final_norm):
    b, s, d = x.shape
    f = w_down.shape[0]
    tf = FFN_CHUNK
    n_chunks = f // tf
    tm = ROW_TILE
    wa = w_up[:, :f].reshape(d, n_chunks, tf)
    wb = w_up[:, f:].reshape(d, n_chunks, tf)
    wup = jnp.concatenate([wa, wb], axis=-1).transpose(1, 0, 2).astype(BF16)
    wdn = w_down.reshape(n_chunks, tf, d).astype(BF16)
    kern = functools.partial(_ffn_kernel, n_chunks=n_chunks, tf=tf, final_norm=final_norm)
    return pl.pallas_call(
        kern,
        grid=(b, s // tm),
        in_specs=[
            _row_spec(tm, d), _batch_vec_spec(d), _batch_vec_spec(d), _batch_vec_spec(d),
            _resident((1, d)), _resident((n_chunks, d, 2 * tf)), _resident((n_chunks, tf, d)),
            _resident((1, d)),
        ],
        out_specs=_row_spec(tm, d),
        out_shape=jax.ShapeDtypeStruct((b, s, d), F32),
        scratch_shapes=[pltpu.VMEM((tm, d), BF16), pltpu.VMEM((tm, d), F32)],
        compiler_params=_params("parallel", "parallel"),
        name="ffn_final" if final_norm else "ffn",
    )(x, shift, scale, gate, norm_g.reshape(1, d), wup, wdn, final_g.reshape(1, d))


def _out_proj_kernel(x_ref, o_ref, gate_ref, w_ref, y_ref, *, feature_major):
    y = _dot_tn(o_ref[0, 0], w_ref[...]) if feature_major else _dot(o_ref[0], w_ref[...])
    y_ref[0] = x_ref[0] + gate_ref[0] * y


def _out_proj(x, o, gate, w_out):
    b, s, d = x.shape
    feature_major = o.ndim == 4
    tm = o.shape[3] if feature_major else ROW_TILE
    o_spec = _block_major_spec(d, tm) if feature_major else _row_spec(tm, d)
    return pl.pallas_call(
        functools.partial(_out_proj_kernel, feature_major=feature_major),
        grid=(b, s // tm),
        in_specs=[_row_spec(tm, d), o_spec, _batch_vec_spec(d), _resident((d, d))],
        out_specs=_row_spec(tm, d),
        out_shape=jax.ShapeDtypeStruct((b, s, d), F32),
        compiler_params=_params("parallel", "parallel"),
        name="out_proj",
    )(x, o, gate, w_out.astype(BF16))


FOX_SLOT = 2 * FOX_HEAD_DIM
ONES_LANE = N_SPLIT * FOX_HEADS
FOX_VROWS = FOX_HEAD_DIM + 16
LOG2E = 1.4426950408889634


def _fox_proj_kernel(x_ref, shift_ref, scale_ref, ng_ref, wqk_ref, wvt_ref, vones_ref, wgt_ref,
                     wf_ref, bf_ref, tri_ref, place_ref, qa_ref, ka_ref, vt_ref, sgt_ref, carry_ref):
    @pl.when(pl.program_id(1) == 0)
    def _():
        carry_ref[...] = jnp.zeros_like(carry_ref)

    h = _modulated_norm(x_ref[0], ng_ref[...], scale_ref[0], shift_ref[0]).astype(BF16)
    z = _dot(h, wf_ref[...]) + bf_ref[...]
    logf = jnp.minimum(z, 0.0) - jnp.log(1.0 + jnp.exp(-jnp.abs(z)))
    pieces = jnp.concatenate(_split_bf16(logf, N_SPLIT), axis=1)
    part = _dot(tri_ref[...], pieces)
    cum = carry_ref[...] + ((part[:, :LANES] + part[:, LANES:2 * LANES]) + part[:, 2 * LANES:])
    carry_ref[...] = cum[-1:, :]

    lane = lax.broadcasted_iota(jnp.int32, cum.shape, 1)
    operand = jnp.where(lane == ONES_LANE, 1.0, 0.0).astype(F32)
    for j, p in enumerate(_split_bf16(cum * LOG2E, N_SPLIT)):
        pf = p.astype(F32)
        shifted = pf if j == 0 else pltpu.roll(pf, j * FOX_HEADS, axis=1)
        operand = jnp.where((lane >= j * FOX_HEADS) & (lane < (j + 1) * FOX_HEADS), shifted, operand)
    operand = operand.astype(BF16)
    width = qa_ref.shape[2]
    qa_ref[0] = (_dot(h, wqk_ref[:, :width]) + _dot(operand, place_ref[:, :width])).astype(BF16)
    ka_ref[0] = (_dot(h, wqk_ref[:, width:]) + _dot(operand, place_ref[:, width:])).astype(BF16)
    vt_ref[0, 0] = (_dot_nt(wvt_ref[...], h) + vones_ref[...]).astype(BF16)
    sgt_ref[0, 0] = _sigmoid(_dot_nt(wgt_ref[...], h)).astype(BF16)


def _fox_placement():
    place = np.zeros((LANES, 2 * FOX_HEADS * FOX_SLOT), np.float32)
    k_off = FOX_HEADS * FOX_SLOT
    for h in range(FOX_HEADS):
        base = h * FOX_SLOT + FOX_HEAD_DIM
        for j in range(N_SPLIT):
            place[j * FOX_HEADS + h, base + j] = 1.0
            place[ONES_LANE, base + N_SPLIT + j] = 1.0
            place[ONES_LANE, k_off + base + j] = 1.0
            place[j * FOX_HEADS + h, k_off + base + N_SPLIT + j] = -1.0
    return jnp.asarray(place, BF16)


def _fox_proj(x, shift, scale, norm_g, w_in, b_f):
    b, s, d = x.shape
    hds, dh = FOX_HEADS, FOX_HEAD_DIM
    tm = ATT_BLOCK
    att_scale = dh ** -0.5

    def slots(w):
        w = w.reshape(d, hds, dh)
        return jnp.concatenate([w, jnp.zeros_like(w)], axis=-1).reshape(d, hds * FOX_SLOT)

    wqk = jnp.concatenate([slots(w_in[:, 0:d] * (att_scale * LOG2E)), slots(w_in[:, d:2 * d])],
                          axis=1).astype(BF16)
    wv = w_in[:, 2 * d:3 * d].T.reshape(hds, dh, d)
    pad = FOX_VROWS - dh
    wvt = jnp.concatenate([wv, jnp.zeros((hds, pad, d), F32)], axis=1).reshape(hds * FOX_VROWS, d).astype(BF16)
    vones = jnp.concatenate([jnp.zeros((hds, dh, 1), F32), jnp.ones((hds, pad, 1), F32)],
                            axis=1).reshape(hds * FOX_VROWS, 1)
    wgt = w_in[:, 3 * d:4 * d].T.astype(BF16)
    wf = jnp.zeros((d, LANES), F32).at[:, :hds].set(w_in[:, 4 * d:]).astype(BF16)
    bf = jnp.zeros((1, LANES), F32).at[0, :hds].set(b_f)
    tri = jnp.tril(jnp.ones((tm, tm), F32)).astype(BF16)
    qw = hds * FOX_SLOT
    vrows = hds * FOX_VROWS
    return pl.pallas_call(
        _fox_proj_kernel,
        grid=(b, s // tm),
        in_specs=[
            _row_spec(tm, d), _batch_vec_spec(d), _batch_vec_spec(d), _resident((1, d)),
            _resident((d, 2 * qw)), _resident((vrows, d)), _resident((vrows, 1)), _resident((d, d)),
            _resident((d, LANES)), _resident((1, LANES)), _resident((tm, tm)), _resident((LANES, 2 * qw)),
        ],
        out_specs=[_row_spec(tm, qw), _row_spec(tm, qw), _block_major_spec(vrows, tm),
                   _block_major_spec(d, tm)],
        out_shape=[
            jax.ShapeDtypeStruct((b, s, qw), BF16), jax.ShapeDtypeStruct((b, s, qw), BF16),
            jax.ShapeDtypeStruct((b, s // tm, vrows, tm), BF16), jax.ShapeDtypeStruct((b, s // tm, d, tm), BF16),
        ],
        scratch_shapes=[pltpu.VMEM((1, LANES), F32)],
        compiler_params=_params("parallel", "arbitrary"),
        name="fox_proj",
    )(x, shift, scale, norm_g.reshape(1, d), wqk, wvt, vones, wgt, wf, bf, tri, _fox_placement())


def _fox_attn_kernel(qa_ref, ka_ref, vt_ref, sgt_ref, o_ref,
                     acc_ref, m_ref, bias_ref, shift_ref, alpha_ref, s_ref, *, blk):
    i = pl.program_id(2)
    dh = FOX_HEAD_DIM
    heads = range(2)

    @pl.when(i == 0)
    def _():
        kp = lax.broadcasted_iota(jnp.int32, (blk, blk), 0)
        qp = lax.broadcasted_iota(jnp.int32, (blk, blk), 1)
        bias_ref[0] = jnp.zeros((blk, blk), F32)
        bias_ref[1] = jnp.where(kp <= qp, 0.0, -jnp.inf)

    acc_ref[...] = jnp.zeros_like(acc_ref)
    m_ref[...] = jnp.full_like(m_ref, -jnp.inf)
    q = qa_ref[0]

    def scores(j, buf):
        k = ka_ref[0, pl.ds(pl.multiple_of(j * blk, blk), blk), :]
        bias = bias_ref[(j == i).astype(jnp.int32)]
        for hh in heads:
            slot = slice(hh * FOX_SLOT, (hh + 1) * FOX_SLOT)
            s = _dot_nt(k[:, slot], q[:, slot]) + bias
            m_old = m_ref[hh]
            m_new = jnp.maximum(m_old, jnp.max(s, axis=0, keepdims=True))
            m_ref[hh] = m_new
            shift_ref[buf, hh] = m_new
            alpha_ref[buf, hh] = jnp.exp2(m_old - m_new)
            s_ref[buf, hh] = s

    def values(j, buf):
        vt = vt_ref[0, j]
        for hh in heads:
            p = jnp.exp2(s_ref[buf, hh] - shift_ref[buf, hh])
            pv = _dot(vt[hh * FOX_VROWS:(hh + 1) * FOX_VROWS, :], p.astype(BF16))
            acc_ref[hh] = alpha_ref[buf, hh] * acc_ref[hh] + pv

    scores(0, 0)

    def body(t, carry):
        scores(2 * t + 1, 1)
        values(2 * t, 0)
        scores(2 * t + 2, 0)
        values(2 * t + 1, 1)
        return carry

    lax.fori_loop(0, i // 2, body, 0)

    @pl.when(i % 2 == 1)
    def _():
        scores(i, 1)
        values(i - 1, 0)
        values(i, 1)

    @pl.when(i % 2 == 0)
    def _():
        values(i, 0)

    for hh in heads:
        rows = slice(hh * dh, (hh + 1) * dh)
        acc = acc_ref[hh]
        gated = acc[:dh] * (1.0 / acc[dh:dh + 1]) * sgt_ref[0, 0, rows, :].astype(F32)
        o_ref[0, 0, rows, :] = gated.astype(BF16)


def _fox_attention(qa, ka, vt, sgt):
    b, nblk, d, blk = sgt.shape
    s = nblk * blk
    pairs = FOX_HEADS // 2
    pair_rows = 2 * FOX_HEAD_DIM
    kern = functools.partial(_fox_attn_kernel, blk=blk)
    return pl.pallas_call(
        kern,
        grid=(b, pairs, nblk),
        in_specs=[
            pl.BlockSpec((1, blk, 2 * FOX_SLOT), lambda bb, p, i: (bb, i, p)),
            pl.BlockSpec((1, s, 2 * FOX_SLOT), lambda bb, p, i: (bb, 0, p)),
            pl.BlockSpec((1, nblk, 2 * FOX_VROWS, blk), lambda bb, p, i: (bb, 0, p, 0)),
            pl.BlockSpec((1, 1, pair_rows, blk), lambda bb, p, i: (bb, i, p, 0)),
        ],
        out_specs=pl.BlockSpec((1, 1, pair_rows, blk), lambda bb, p, i: (bb, i, p, 0)),
        out_shape=jax.ShapeDtypeStruct((b, nblk, d, blk), BF16),
        scratch_shapes=[
            pltpu.VMEM((2, FOX_VROWS, blk), F32),
            pltpu.VMEM((2, 1, blk), F32),
            pltpu.VMEM((2, blk, blk), F32),
            pltpu.VMEM((2, 2, 1, blk), F32),
            pltpu.VMEM((2, 2, 1, blk), F32),
            pltpu.VMEM((2, 2, blk, blk), F32),
        ],
        compiler_params=_params("parallel", "parallel", "arbitrary"),
        name="fox_attention",
    )(qa, ka, vt, sgt)


def _hgrn_proj_kernel(x_ref, shift_ref, scale_ref, ng_ref, w_ref, lbl_ref,
                      q_ref, kk_ref, logf_ref, v_ref, sg_ref, *, d, layer):
    lg = lbl_ref[...]
    e = jnp.exp(lg - jnp.max(lg, axis=0, keepdims=True))
    sm = e / jnp.sum(e, axis=0, keepdims=True)
    lb = jnp.sum(sm[0:layer + 1], axis=0, keepdims=True) - sm[0:1]

    h = _modulated_norm(x_ref[0], ng_ref[...], scale_ref[0], shift_ref[0]).astype(BF16)
    proj = _dot(h, w_ref[...])
    f = lb + (1.0 - lb) * _sigmoid(proj[:, d:2 * d])
    q_ref[0] = proj[:, 0:d].astype(BF16)
    kk_ref[0] = (1.0 - f).astype(BF16)
    logf_ref[0] = jnp.log(f)
    v_ref[0] = _silu(proj[:, 2 * d:3 * d]).astype(BF16)
    sg_ref[0] = _silu(proj[:, 3 * d:4 * d]).astype(BF16)


def _hgrn_proj(x, shift, scale, norm_g, w_in, lb_logits, layer):
    b, s, d = x.shape
    depth = lb_logits.shape[0]
    tm = ROW_TILE // 2
    kern = functools.partial(_hgrn_proj_kernel, d=d, layer=layer)
    bf = jax.ShapeDtypeStruct((b, s, d), BF16)
    return pl.pallas_call(
        kern,
        grid=(b, s // tm),
        in_specs=[
            _row_spec(tm, d), _batch_vec_spec(d), _batch_vec_spec(d), _resident((1, d)),
            _resident((d, 4 * d)), _resident((depth, d)),
        ],
        out_specs=[_row_spec(tm, d)] * 5,
        out_shape=[bf, bf, jax.ShapeDtypeStruct((b, s, d), F32), bf, bf],
        compiler_params=_params("parallel", "parallel"),
        name="hgrn_proj",
    )(x, shift, scale, norm_g.reshape(1, d), w_in.astype(BF16), lb_logits.astype(F32))


def _prefix_rows(x):
    n = x.shape[0]
    row = lax.broadcasted_iota(jnp.int32, x.shape, 0)
    shift = 1
    while shift < n:
        x = x + jnp.where(row >= shift, pltpu.roll(x, shift, axis=0), 0.0)
        shift *= 2
    return x


def _pivot_rows(g, half):
    n, lanes = g.shape
    if 2 * half >= SUBLANES:
        g3 = g.reshape(n // (2 * half), 2 * half, lanes)
        piv = jnp.broadcast_to(g3[:, half - 1:half, :], g3.shape)
        return piv.reshape(n, lanes)
    g3 = g.reshape(n // SUBLANES, SUBLANES, lanes)
    sub = lax.broadcasted_iota(jnp.int32, g3.shape, 1)
    out = None
    for grp in range(SUBLANES // (2 * half)):
        r = grp * 2 * half + half - 1
        cand = jnp.broadcast_to(g3[:, r:r + 1, :], g3.shape)
        out = cand if out is None else jnp.where(sub >= grp * 2 * half, cand, out)
    return out.reshape(n, lanes)


def _hgrn_chunk_kernel(q_ref, kk_ref, logf_ref, v_ref, sg_ref, gn_ref, o_ref, state_ref,
                       *, chunk, heads):
    @pl.when(pl.program_id(2) == 0)
    def _():
        state_ref[...] = jnp.zeros_like(state_ref)

    kdim = HGRN_EXPAND
    t_idx = lax.broadcasted_iota(jnp.int32, (chunk, chunk), 0)
    s_idx = lax.broadcasted_iota(jnp.int32, (chunk, chunk), 1)
    row = lax.broadcasted_iota(jnp.int32, (chunk, kdim), 0)
    for hh in range(heads):
        cols = slice(hh * kdim, (hh + 1) * kdim)
        q = q_ref[0, :, cols].astype(F32)
        kk = kk_ref[0, :, cols].astype(F32)
        v = v_ref[0, :, cols]
        g = _prefix_rows(logf_ref[0, :, cols])

        a = jnp.where(t_idx == s_idx, jnp.sum(q * kk, axis=-1, keepdims=True), 0.0)
        half = chunk // 2
        while half >= 1:
            e = jnp.exp(-jnp.abs(g - _pivot_rows(g, half)))
            upper = (row & half) != 0
            qe = jnp.where(upper, q * e, 0.0).astype(BF16)
            ke = jnp.where(upper, 0.0, kk * e).astype(BF16)
            same_block = (t_idx ^ s_idx) < 2 * half
            a = a + jnp.where(same_block, _dot_nt(qe, ke), 0.0)
            half //= 2

        g_last = g[chunk - 1:chunk, :]
        state_t = state_ref[hh]
        o = _dot_nt((q * jnp.exp(g)).astype(BF16), state_t.astype(BF16)) + _dot(a.astype(BF16), v)
        k_tail = (kk * jnp.exp(g_last - g)).astype(BF16)
        state_ref[hh] = state_t * jnp.exp(g_last) + _dot_tn(v, k_tail)

        o = o * lax.rsqrt(jnp.mean(o * o, axis=-1, keepdims=True) + EPS)
        o = (o * gn_ref[:, cols]) * sg_ref[0, :, cols].astype(F32)
        o_ref[0, :, cols] = o.astype(BF16)


def _hgrn_chunks(q, kk, logf, v, sg, g_norm):
    b, s, d = q.shape
    chunk, heads = HGRN_CHUNK, HGRN_HEADS_PER_STEP
    width = heads * HGRN_EXPAND
    spec = pl.BlockSpec((1, chunk, width), lambda bb, hp, c: (bb, c, hp))
    kern = functools.partial(_hgrn_chunk_kernel, chunk=chunk, heads=heads)
    return pl.pallas_call(
        kern,
        grid=(b, d // width, s // chunk),
        in_specs=[spec] * 5 + [pl.BlockSpec((1, width), lambda bb, hp, c: (0, hp))],
        out_specs=spec,
        out_shape=jax.ShapeDtypeStruct((b, s, d), BF16),
        scratch_shapes=[pltpu.VMEM((heads, HGRN_EXPAND, HGRN_EXPAND), F32)],
        compiler_params=_params("parallel", "parallel", "arbitrary"),
        name="hgrn_chunks",
    )(q, kk, logf, v, sg, g_norm.reshape(1, d).astype(F32))


def kernel(x, c, ada_w, ada_b, norm_g, ffn_w_up, ffn_w_down, fox_w_in, fox_b_f, fox_w_out,
           hgrn_w_in, hgrn_norm_g, hgrn_w_out, hgrn_lb_logits, final_norm_g):
    b, s, d = x.shape
    depth = ada_w.shape[0]
    mod = _ada_modulation(c, ada_w, ada_b).reshape(depth, b, N_SUB, 3, 1, d)
    for i in range(depth):
        shift = lambda sub: mod[i, :, sub, 0]
        scale = lambda sub: mod[i, :, sub, 1]
        gate = lambda sub: mod[i, :, sub, 2]
        x = _ffn(x, shift(0), scale(0), gate(0), norm_g[i, 0], ffn_w_up[i, 0], ffn_w_down[i, 0],
                 final_norm_g, False)
        j = i // 2
        if i % 2 == 0:
            qa, ka, vt, sgt = _fox_proj(x, shift(1), scale(1), norm_g[i, 1], fox_w_in[j], fox_b_f[j])
            o = _fox_attention(qa, ka, vt, sgt)
            x = _out_proj(x, o, gate(1), fox_w_out[j])
        else:
            q, kk, logf, v, sg = _hgrn_proj(x, shift(1), scale(1), norm_g[i, 1], hgrn_w_in[j],
                                            hgrn_lb_logits, i)
            o = _hgrn_chunks(q, kk, logf, v, sg, hgrn_norm_g[j])
            x = _out_proj(x, o, gate(1), hgrn_w_out[j])
        x = _ffn(x, shift(2), scale(2), gate(2), norm_g[i, 2], ffn_w_up[i, 1], ffn_w_down[i, 1],
                 final_norm_g, i == depth - 1)
    return x
```

```python
import functools

import jax
import jax.numpy as jnp
import numpy as np
from jax import lax
from jax.experimental import pallas as pl
from jax.experimental.pallas import tpu as pltpu

N_SUB = 3
FOX_HEADS = 16
FOX_HEAD_DIM = 64
HGRN_EXPAND = 128
EPS = 1e-6

LANES = 128
SUBLANES = 8
VMEM_LIMIT_BYTES = 56 * 1024 * 1024

ROW_TILE = 1024
FFN_CHUNK = 256
ATT_BLOCK = 512
HGRN_CHUNK = 128
HGRN_HEADS_PER_STEP = 2
N_SPLIT = 3

BF16 = jnp.bfloat16
F32 = jnp.float32


def _params(*sem):
    return pltpu.CompilerParams(dimension_semantics=sem, vmem_limit_bytes=VMEM_LIMIT_BYTES)


def _resident(shape):
    zeros = (0,) * len(shape)
    return pl.BlockSpec(shape, lambda *_: zeros, pipeline_mode=pl.Buffered(1))


def _row_spec(tm, width):
    return pl.BlockSpec((1, tm, width), lambda b, i: (b, i, 0))


def _block_major_spec(features, tm):
    return pl.BlockSpec((1, 1, features, tm), lambda b, i: (b, i, 0, 0))


def _batch_vec_spec(width):
    return pl.BlockSpec((1, 1, width), lambda b, i: (b, 0, 0))


def _dot(a, b):
    return jnp.dot(a, b, preferred_element_type=F32)


def _dot_nt(a, b):
    return lax.dot_general(a, b, (((1,), (1,)), ((), ())), preferred_element_type=F32)


def _dot_tn(a, b):
    return lax.dot_general(a, b, (((0,), (0,)), ((), ())), preferred_element_type=F32)


def _sigmoid(x):
    return 1.0 / (1.0 + jnp.exp(-x))


def _silu(x):
    return x * _sigmoid(x)


def _modulated_norm(x, norm_g, scale, shift):
    y = x * lax.rsqrt(jnp.mean(x * x, axis=-1, keepdims=True) + EPS)
    return (y * norm_g) * (1.0 + scale) + shift


def _split_bf16(x, n):
    pieces = []
    rest = x
    for _ in range(n):
        p = rest.astype(BF16)
        pieces.append(p)
        rest = rest - p.astype(F32)
    return pieces


def _ada_kernel(c_ref, w_ref, b_ref, o_ref):
    cond = _silu(c_ref[...])
    y = jnp.dot(cond, w_ref[0], preferred_element_type=F32,
                precision=lax.Precision.HIGHEST)
    o_ref[0] = y + b_ref[0]


def _ada_modulation(c, ada_w, ada_b):
    depth, d, n = ada_w.shape
    b = c.shape[0]
    bp = -(-b // SUBLANES) * SUBLANES
    tn = n // 9 if n % (9 * LANES) == 0 else n
    c_pad = jnp.zeros((bp, d), F32).at[:b].set(c)
    out = pl.pallas_call(
        _ada_kernel,
        grid=(depth, n // tn),
        in_specs=[
            pl.BlockSpec((bp, d), lambda i, j: (0, 0)),
            pl.BlockSpec((1, d, tn), lambda i, j: (i, 0, j)),
            pl.BlockSpec((1, 1, tn), lambda i, j: (i, 0, j)),
        ],
        out_specs=pl.BlockSpec((1, bp, tn), lambda i, j: (i, 0, j)),
        out_shape=jax.ShapeDtypeStruct((depth, bp, n), F32),
        compiler_params=_params("parallel", "parallel"),
        name="ada_modulation",
    )(c_pad, ada_w, ada_b.reshape(depth, 1, n))
    return out[:, :b]


def _ffn_kernel(x_ref, shift_ref, scale_ref, gate_ref, ng_ref, wup_ref, wdn_ref, fg_ref,
                o_ref, h_ref, acc_ref, *, n_chunks, tf, final_norm):
    x = x_ref[0]
    h_ref[...] = _modulated_norm(x, ng_ref[...], scale_ref[0], shift_ref[0]).astype(BF16)
    acc_ref[...] = jnp.zeros_like(acc_ref)

    def body(f, carry):
        ab = _dot(h_ref[...], wup_ref[f])
        g = _silu(ab[:, :tf]) * ab[:, tf:]
        acc_ref[...] += _dot(g.astype(BF16), wdn_ref[f])
        return carry

    lax.fori_loop(0, n_chunks, body, 0, unroll=True)
    y = x + (0.5 * gate_ref[0]) * acc_ref[...]
    if final_norm:
        y = (y * lax.rsqrt(jnp.mean(y * y, axis=-1, keepdims=True) + EPS)) * fg_ref[...]
    o_ref[0] = y


def _ffn(x, shift, scale, gate, norm_g, w_up, w_down, final_g, final_norm):
    b, s, d = x.shape
    f = w_down.shape[0]
    tf = FFN_CHUNK
    n_chunks = f // tf
    tm = ROW_TILE
    wa = w_up[:, :f].reshape(d, n_chunks, tf)
    wb = w_up[:, f:].reshape(d, n_chunks, tf)
    wup = jnp.concatenate([wa, wb], axis=-1).transpose(1, 0, 2).astype(BF16)
    wdn = w_down.reshape(n_chunks, tf, d).astype(BF16)
    kern = functools.partial(_ffn_kernel, n_chunks=n_chunks, tf=tf, final_norm=final_norm)
    return pl.pallas_call(
        kern,
        grid=(b, s // tm),
        in_specs=[
            _row_spec(tm, d), _batch_vec_spec(d), _batch_vec_spec(d), _batch_vec_spec(d),
            _resident((1, d)), _resident((n_chunks, d, 2 * tf)), _resident((n_chunks, tf, d)),
            _resident((1, d)),
        ],
        out_specs=_row_spec(tm, d),
        out_shape=jax.ShapeDtypeStruct((b, s, d), F32),
        scratch_shapes=[pltpu.VMEM((tm, d), BF16), pltpu.VMEM((tm, d), F32)],
        compiler_params=_params("parallel", "parallel"),
        name="ffn_final" if final_norm else "ffn",
    )(x, shift, scale, gate, norm_g.reshape(1, d), wup, wdn, final_g.reshape(1, d))


def _out_proj_kernel(x_ref, o_ref, gate_ref, w_ref, y_ref, *, feature_major):
    y = _dot_tn(o_ref[0, 0], w_ref[...]) if feature_major else _dot(o_ref[0], w_ref[...])
    y_ref[0] = x_ref[0] + gate_ref[0] * y


def _out_proj(x, o, gate, w_out):
    b, s, d = x.shape
    feature_major = o.ndim == 4
    tm = o.shape[3] if feature_major else ROW_TILE
    o_spec = _block_major_spec(d, tm) if feature_major else _row_spec(tm, d)
    return pl.pallas_call(
        functools.partial(_out_proj_kernel, feature_major=feature_major),
        grid=(b, s // tm),
        in_specs=[_row_spec(tm, d), o_spec, _batch_vec_spec(d), _resident((d, d))],
        out_specs=_row_spec(tm, d),
        out_shape=jax.ShapeDtypeStruct((b, s, d), F32),
        compiler_params=_params("parallel", "parallel"),
        name="out_proj",
    )(x, o, gate, w_out.astype(BF16))


FOX_SLOT = 2 * FOX_HEAD_DIM
ONES_LANE = N_SPLIT * FOX_HEADS
FOX_VROWS = FOX_HEAD_DIM + 16
LOG2E = 1.4426950408889634


def _fox_proj_kernel(x_ref, shift_ref, scale_ref, ng_ref, wqk_ref, wvt_ref, vones_ref, wgt_ref,
                     wf_ref, bf_ref, tri_ref, place_ref, qa_ref, ka_ref, vt_ref, sgt_ref, carry_ref):
    @pl.when(pl.program_id(1) == 0)
    def _():
        carry_ref[...] = jnp.zeros_like(carry_ref)

    h = _modulated_norm(x_ref[0], ng_ref[...], scale_ref[0], shift_ref[0]).astype(BF16)
    z = _dot(h, wf_ref[...]) + bf_ref[...]
    logf = jnp.minimum(z, 0.0) - jnp.log(1.0 + jnp.exp(-jnp.abs(z)))
    pieces = jnp.concatenate(_split_bf16(logf, N_SPLIT), axis=1)
    part = _dot(tri_ref[...], pieces)
    cum = carry_ref[...] + ((part[:, :LANES] + part[:, LANES:2 * LANES]) + part[:, 2 * LANES:])
    carry_ref[...] = cum[-1:, :]

    lane = lax.broadcasted_iota(jnp.int32, cum.shape, 1)
    operand = jnp.where(lane == ONES_LANE, 1.0, 0.0).astype(F32)
    for j, p in enumerate(_split_bf16(cum * LOG2E, N_SPLIT)):
        pf = p.astype(F32)
        shifted = pf if j == 0 else pltpu.roll(pf, j * FOX_HEADS, axis=1)
        operand = jnp.where((lane >= j * FOX_HEADS) & (lane < (j + 1) * FOX_HEADS), shifted, operand)
    operand = operand.astype(BF16)
    width = qa_ref.shape[2]
    qa_ref[0] = (_dot(h, wqk_ref[:, :width]) + _dot(operand, place_ref[:, :width])).astype(BF16)
    ka_ref[0] = (_dot(h, wqk_ref[:, width:]) + _dot(operand, place_ref[:, width:])).astype(BF16)
    vt_ref[0, 0] = (_dot_nt(wvt_ref[...], h) + vones_ref[...]).astype(BF16)
    sgt_ref[0, 0] = _sigmoid(_dot_nt(wgt_ref[...], h)).astype(BF16)


def _fox_placement():
    place = np.zeros((LANES, 2 * FOX_HEADS * FOX_SLOT), np.float32)
    k_off = FOX_HEADS * FOX_SLOT
    for h in range(FOX_HEADS):
        base = h * FOX_SLOT + FOX_HEAD_DIM
        for j in range(N_SPLIT):
            place[j * FOX_HEADS + h, base + j] = 1.0
            place[ONES_LANE, base + N_SPLIT + j] = 1.0
            place[ONES_LANE, k_off + base + j] = 1.0
            place[j * FOX_HEADS + h, k_off + base + N_SPLIT + j] = -1.0
    return jnp.asarray(place, BF16)


def _fox_proj(x, shift, scale, norm_g, w_in, b_f):
    b, s, d = x.shape
    hds, dh = FOX_HEADS, FOX_HEAD_DIM
    tm = ATT_BLOCK
    att_scale = dh ** -0.5

    def slots(w):
        w = w.reshape(d, hds, dh)
        return jnp.concatenate([w, jnp.zeros_like(w)], axis=-1).reshape(d, hds * FOX_SLOT)

    wqk = jnp.concatenate([slots(w_in[:, 0:d] * (att_scale * LOG2E)), slots(w_in[:, d:2 * d])],
                          axis=1).astype(BF16)
    wv = w_in[:, 2 * d:3 * d].T.reshape(hds, dh, d)
    pad = FOX_VROWS - dh
    wvt = jnp.concatenate([wv, jnp.zeros((hds, pad, d), F32)], axis=1).reshape(hds * FOX_VROWS, d).astype(BF16)
    vones = jnp.concatenate([jnp.zeros((hds, dh, 1), F32), jnp.ones((hds, pad, 1), F32)],
                            axis=1).reshape(hds * FOX_VROWS, 1)
    wgt = w_in[:, 3 * d:4 * d].T.astype(BF16)
    wf = jnp.zeros((d, LANES), F32).at[:, :hds].set(w_in[:, 4 * d:]).astype(BF16)
    bf = jnp.zeros((1, LANES), F32).at[0, :hds].set(b_f)
    tri = jnp.tril(jnp.ones((tm, tm), F32)).astype(BF16)
    qw = hds * FOX_SLOT
    vrows = hds * FOX_VROWS
    return pl.pallas_call(
        _fox_proj_kernel,
        grid=(b, s // tm),
        in_specs=[
            _row_spec(tm, d), _batch_vec_spec(d), _batch_vec_spec(d), _resident((1, d)),
            _resident((d, 2 * qw)), _resident((vrows, d)), _resident((vrows, 1)), _resident((d, d)),
            _resident((d, LANES)), _resident((1, LANES)), _resident((tm, tm)), _resident((LANES, 2 * qw)),
        ],
        out_specs=[_row_spec(tm, qw), _row_spec(tm, qw), _block_major_spec(vrows, tm),
                   _block_major_spec(d, tm)],
        out_shape=[
            jax.ShapeDtypeStruct((b, s, qw), BF16), jax.ShapeDtypeStruct((b, s, qw), BF16),
            jax.ShapeDtypeStruct((b, s // tm, vrows, tm), BF16), jax.ShapeDtypeStruct((b, s // tm, d, tm), BF16),
        ],
        scratch_shapes=[pltpu.VMEM((1, LANES), F32)],
        compiler_params=_params("parallel", "arbitrary"),
        name="fox_proj",
    )(x, shift, scale, norm_g.reshape(1, d), wqk, wvt, vones, wgt, wf, bf, tri, _fox_placement())


def _fox_attn_kernel(qa_ref, ka_ref, vt_ref, sgt_ref, o_ref,
                     acc_ref, m_ref, bias_ref, shift_ref, alpha_ref, s_ref, *, blk, nblk):
    dh = FOX_HEAD_DIM
    heads = range(2)

    kp = lax.broadcasted_iota(jnp.int32, (blk, blk), 0)
    qp = lax.broadcasted_iota(jnp.int32, (blk, blk), 1)
    bias_ref[...] = jnp.where(kp <= qp, 0.0, -jnp.inf)
    acc_ref[...] = jnp.zeros_like(acc_ref)
    m_ref[...] = jnp.full_like(m_ref, -jnp.inf)

    def scores(ij, buf, diagonal):
        i, j = ij
        q = qa_ref[0, pl.ds(pl.multiple_of(i * blk, blk), blk), :]
        k = ka_ref[0, pl.ds(pl.multiple_of(j * blk, blk), blk), :]
        for hh in heads:
            slot = slice(hh * FOX_SLOT, (hh + 1) * FOX_SLOT)
            s = _dot_nt(k[:, slot], q[:, slot])
            if diagonal:
                s = s + bias_ref[...]
            m_old = m_ref[i, hh]
            m_new = jnp.maximum(m_old, jnp.max(s, axis=0, keepdims=True))
            m_ref[i, hh] = m_new
            shift_ref[buf, hh] = m_new
            alpha_ref[buf, hh] = jnp.exp2(m_old - m_new)
            s_ref[buf, hh] = s

    def values(ij, buf, diagonal):
        i, j = ij
        vt = vt_ref[0, j]
        for hh in heads:
            p = jnp.exp2(s_ref[buf, hh] - shift_ref[buf, hh])
            pv = _dot(vt[hh * FOX_VROWS:(hh + 1) * FOX_VROWS, :], p.astype(BF16))
            acc = alpha_ref[buf, hh] * acc_ref[i, hh] + pv
            if diagonal:
                rows = slice(hh * dh, (hh + 1) * dh)
                gated = acc[:dh] * (1.0 / acc[dh:dh + 1]) * sgt_ref[0, i, rows, :].astype(F32)
                o_ref[0, i, rows, :] = gated.astype(BF16)
            else:
                acc_ref[i, hh] = acc

    def sweep(first, advance, n_pairs, diagonal):
        assert n_pairs % 2 == 0 and n_pairs >= 4

        def body(t, carry):
            sc, va = carry
            scores(sc, 1, diagonal)
            values(va, 0, diagonal)
            sc, va = advance(sc), advance(va)
            scores(sc, 0, diagonal)
            values(va, 1, diagonal)
            return advance(sc), advance(va)

        scores(first, 0, diagonal)
        sc, va = lax.fori_loop(0, n_pairs // 2 - 1, body, (advance(first), first))
        scores(sc, 1, diagonal)
        values(va, 0, diagonal)
        values(advance(va), 1, diagonal)

    def next_below_diagonal(ij):
        i, j = ij
        wrap = j + 1 == i
        return jnp.where(wrap, i + 1, i), jnp.where(wrap, 0, j + 1)

    zero, one = jnp.int32(0), jnp.int32(1)
    sweep((one, zero), next_below_diagonal, nblk * (nblk - 1) // 2, False)
    sweep((zero, zero), lambda ij: (ij[0] + 1, ij[1] + 1), nblk, True)


def _fox_attention(qa, ka, vt, sgt):
    b, nblk, d, blk = sgt.shape
    s = nblk * blk
    pairs = FOX_HEADS // 2
    pair_rows = 2 * FOX_HEAD_DIM
    kern = functools.partial(_fox_attn_kernel, blk=blk, nblk=nblk)
    return pl.pallas_call(
        kern,
        grid=(b, pairs),
        in_specs=[
            pl.BlockSpec((1, s, 2 * FOX_SLOT), lambda bb, p: (bb, 0, p)),
            pl.BlockSpec((1, s, 2 * FOX_SLOT), lambda bb, p: (bb, 0, p)),
            pl.BlockSpec((1, nblk, 2 * FOX_VROWS, blk), lambda bb, p: (bb, 0, p, 0)),
            pl.BlockSpec((1, nblk, pair_rows, blk), lambda bb, p: (bb, 0, p, 0)),
        ],
        out_specs=pl.BlockSpec((1, nblk, pair_rows, blk), lambda bb, p: (bb, 0, p, 0)),
        out_shape=jax.ShapeDtypeStruct((b, nblk, d, blk), BF16),
        scratch_shapes=[
            pltpu.VMEM((nblk, 2, FOX_VROWS, blk), F32),
            pltpu.VMEM((nblk, 2, 1, blk), F32),
            pltpu.VMEM((blk, blk), F32),
            pltpu.VMEM((2, 2, 1, blk), F32),
            pltpu.VMEM((2, 2, 1, blk), F32),
            pltpu.VMEM((2, 2, blk, blk), F32),
        ],
        compiler_params=_params("parallel", "parallel"),
        name="fox_attention",
    )(qa, ka, vt, sgt)


def _hgrn_proj_kernel(x_ref, shift_ref, scale_ref, ng_ref, w_ref, lbl_ref,
                      q_ref, kk_ref, logf_ref, v_ref, sg_ref, *, d, layer):
    lg = lbl_ref[...]
    e = jnp.exp(lg - jnp.max(lg, axis=0, keepdims=True))
    sm = e / jnp.sum(e, axis=0, keepdims=True)
    lb = jnp.sum(sm[0:layer + 1], axis=0, keepdims=True) - sm[0:1]

    h = _modulated_norm(x_ref[0], ng_ref[...], scale_ref[0], shift_ref[0]).astype(BF16)
    proj = _dot(h, w_ref[...])
    f = lb + (1.0 - lb) * _sigmoid(proj[:, d:2 * d])
    q_ref[0] = proj[:, 0:d].astype(BF16)
    kk_ref[0] = (1.0 - f).astype(BF16)
    logf_ref[0] = jnp.log(f)
    v_ref[0] = _silu(proj[:, 2 * d:3 * d]).astype(BF16)
    sg_ref[0] = _silu(proj[:, 3 * d:4 * d]).astype(BF16)


def _hgrn_proj(x, shift, scale, norm_g, w_in, lb_logits, layer):
    b, s, d = x.shape
    depth = lb_logits.shape[0]
    tm = ROW_TILE // 2
    kern = functools.partial(_hgrn_proj_kernel, d=d, layer=layer)
    bf = jax.ShapeDtypeStruct((b, s, d), BF16)
    return pl.pallas_call(
        kern,
        grid=(b, s // tm),
        in_specs=[
            _row_spec(tm, d), _batch_vec_spec(d), _batch_vec_spec(d), _resident((1, d)),
            _resident((d, 4 * d)), _resident((depth, d)),
        ],
        out_specs=[_row_spec(tm, d)] * 5,
        out_shape=[bf, bf, jax.ShapeDtypeStruct((b, s, d), F32), bf, bf],
        compiler_params=_params("parallel", "parallel"),
        name="hgrn_proj",
    )(x, shift, scale, norm_g.reshape(1, d), w_in.astype(BF16), lb_logits.astype(F32))


def _prefix_rows(x):
    n = x.shape[0]
    row = lax.broadcasted_iota(jnp.int32, x.shape, 0)
    shift = 1
    while shift < n:
        x = x + jnp.where(row >= shift, pltpu.roll(x, shift, axis=0), 0.0)
        shift *= 2
    return x


def _pivot_rows(g, half):
    n, lanes = g.shape
    if 2 * half >= SUBLANES:
        g3 = g.reshape(n // (2 * half), 2 * half, lanes)
        piv = jnp.broadcast_to(g3[:, half - 1:half, :], g3.shape)
        return piv.reshape(n, lanes)
    g3 = g.reshape(n // SUBLANES, SUBLANES, lanes)
    sub = lax.broadcasted_iota(jnp.int32, g3.shape, 1)
    out = None
    for grp in range(SUBLANES // (2 * half)):
        r = grp * 2 * half + half - 1
        cand = jnp.broadcast_to(g3[:, r:r + 1, :], g3.shape)
        out = cand if out is None else jnp.where(sub >= grp * 2 * half, cand, out)
    return out.reshape(n, lanes)


def _hgrn_chunk_kernel(q_ref, kk_ref, logf_ref, v_ref, sg_ref, gn_ref, o_ref, state_ref,
                       *, chunk, heads):
    @pl.when(pl.program_id(2) == 0)
    def _():
        state_ref[...] = jnp.zeros_like(state_ref)

    kdim = HGRN_EXPAND
    t_idx = lax.broadcasted_iota(jnp.int32, (chunk, chunk), 0)
    s_idx = lax.broadcasted_iota(jnp.int32, (chunk, chunk), 1)
    row = lax.broadcasted_iota(jnp.int32, (chunk, kdim), 0)
    for hh in range(heads):
        cols = slice(hh * kdim, (hh + 1) * kdim)
        q = q_ref[0, :, cols].astype(F32)
        kk = kk_ref[0, :, cols].astype(F32)
        v = v_ref[0, :, cols]
        g = _prefix_rows(logf_ref[0, :, cols])

        a = jnp.where(t_idx == s_idx, jnp.sum(q * kk, axis=-1, keepdims=True), 0.0)
        half = chunk // 2
        while half >= 1:
            e = jnp.exp(-jnp.abs(g - _pivot_rows(g, half)))
            upper = (row & half) != 0
            qe = jnp.where(upper, q * e, 0.0).astype(BF16)
            ke = jnp.where(upper, 0.0, kk * e).astype(BF16)
            same_block = (t_idx ^ s_idx) < 2 * half
            a = a + jnp.where(same_block, _dot_nt(qe, ke), 0.0)
            half //= 2

        g_last = g[chunk - 1:chunk, :]
        state_t = state_ref[hh]
        o = _dot_nt((q * jnp.exp(g)).astype(BF16), state_t.astype(BF16)) + _dot(a.astype(BF16), v)
        k_tail = (kk * jnp.exp(g_last - g)).astype(BF16)
        state_ref[hh] = state_t * jnp.exp(g_last) + _dot_tn(v, k_tail)

        o = o * lax.rsqrt(jnp.mean(o * o, axis=-1, keepdims=True) + EPS)
        o = (o * gn_ref[:, cols]) * sg_ref[0, :, cols].astype(F32)
        o_ref[0, :, cols] = o.astype(BF16)


def _hgrn_chunks(q, kk, logf, v, sg, g_norm):
    b, s, d = q.shape
    chunk, heads = HGRN_CHUNK, HGRN_HEADS_PER_STEP
    width = heads * HGRN_EXPAND
    spec = pl.BlockSpec((1, chunk, width), lambda bb, hp, c: (bb, c, hp))
    kern = functools.partial(_hgrn_chunk_kernel, chunk=chunk, heads=heads)
    return pl.pallas_call(
        kern,
        grid=(b, d // width, s // chunk),
        in_specs=[spec] * 5 + [pl.BlockSpec((1, width), lambda bb, hp, c: (0, hp))],
        out_specs=spec,
        out_shape=jax.ShapeDtypeStruct((b, s, d), BF16),
        scratch_shapes=[pltpu.VMEM((heads, HGRN_EXPAND, HGRN_EXPAND), F32)],
        compiler_params=_params("parallel", "parallel", "arbitrary"),
        name="hgrn_chunks",
    )(q, kk, logf, v, sg, g_norm.reshape(1, d).astype(F32))


def kernel(x, c, ada_w, ada_b, norm_g, ffn_w_up, ffn_w_down, fox_w_in, fox_b_f, fox_w_out,
           hgrn_w_in, hgrn_norm_g, hgrn_w_out, hgrn_lb_logits, final_norm_g):
    b, s, d = x.shape
    depth = ada_w.shape[0]
    mod = _ada_modulation(c, ada_w, ada_b).reshape(depth, b, N_SUB, 3, 1, d)
    for i in range(depth):
        shift = lambda sub: mod[i, :, sub, 0]
        scale = lambda sub: mod[i, :, sub, 1]
        gate = lambda sub: mod[i, :, sub, 2]
        x = _ffn(x, shift(0), scale(0), gate(0), norm_g[i, 0], ffn_w_up[i, 0], ffn_w_down[i, 0],
                 final_norm_g, False)
        j = i // 2
        if i % 2 == 0:
            qa, ka, vt, sgt = _fox_proj(x, shift(1), scale(1), norm_g[i, 1], fox_w_in[j], fox_b_f[j])
            o = _fox_attention(qa, ka, vt, sgt)
            x = _out_proj(x, o, gate(1), fox_w_out[j])
        else:
            q, kk, logf, v, sg = _hgrn_proj(x, shift(1), scale(1), norm_g[i, 1], hgrn_w_in[j],
                                            hgrn_lb_logits, i)
            o = _hgrn_chunks(q, kk, logf, v, sg, hgrn_norm_g[j])
            x = _out_proj(x, o, gate(1), hgrn_w_out[j])
        x = _ffn(x, shift(2), scale(2), gate(2), norm_g[i, 2], ffn_w_up[i, 1], ffn_w_down[i, 1],
                 final_norm_g, i == depth - 1)
    return x
```

```python
import functools

import jax
import jax.numpy as jnp
import numpy as np
from jax import lax
from jax.experimental import pallas as pl
from jax.experimental.pallas import tpu as pltpu

N_SUB = 3
FOX_HEADS = 16
FOX_HEAD_DIM = 64
HGRN_EXPAND = 128
EPS = 1e-6

LANES = 128
SUBLANES = 8
VMEM_LIMIT_BYTES = 56 * 1024 * 1024

ROW_TILE = 1024
FFN_CHUNK = 256
ATT_BLOCK = 512
HGRN_CHUNK = 128
HGRN_HEADS_PER_STEP = 8
N_SPLIT = 3

BF16 = jnp.bfloat16
F32 = jnp.float32


def _params(*sem):
    return pltpu.CompilerParams(dimension_semantics=sem, vmem_limit_bytes=VMEM_LIMIT_BYTES)


def _resident(shape):
    zeros = (0,) * len(shape)
    return pl.BlockSpec(shape, lambda *_: zeros, pipeline_mode=pl.Buffered(1))


def _row_spec(tm, width):
    return pl.BlockSpec((1, tm, width), lambda b, i: (b, i, 0))


def _block_major_spec(features, tm):
    return pl.BlockSpec((1, 1, features, tm), lambda b, i: (b, i, 0, 0))


def _batch_vec_spec(width):
    return pl.BlockSpec((1, 1, width), lambda b, i: (b, 0, 0))


def _dot(a, b):
    return jnp.dot(a, b, preferred_element_type=F32)


def _dot_nt(a, b):
    return lax.dot_general(a, b, (((1,), (1,)), ((), ())), preferred_element_type=F32)


def _dot_tn(a, b):
    return lax.dot_general(a, b, (((0,), (0,)), ((), ())), preferred_element_type=F32)


def _sigmoid(x):
    return 1.0 / (1.0 + jnp.exp(-x))


def _silu(x):
    return x * _sigmoid(x)


def _modulated_norm(x, norm_g, scale, shift):
    y = x * lax.rsqrt(jnp.mean(x * x, axis=-1, keepdims=True) + EPS)
    return (y * norm_g) * (1.0 + scale) + shift


def _split_bf16(x, n):
    pieces = []
    rest = x
    for _ in range(n):
        p = rest.astype(BF16)
        pieces.append(p)
        rest = rest - p.astype(F32)
    return pieces


def _ada_kernel(c_ref, w_ref, b_ref, o_ref):
    cond = _silu(c_ref[...])
    y = jnp.dot(cond, w_ref[0], preferred_element_type=F32,
                precision=lax.Precision.HIGHEST)
    o_ref[0] = y + b_ref[0]


def _ada_modulation(c, ada_w, ada_b):
    depth, d, n = ada_w.shape
    b = c.shape[0]
    bp = -(-b // SUBLANES) * SUBLANES
    tn = n // 9 if n % (9 * LANES) == 0 else n
    c_pad = jnp.zeros((bp, d), F32).at[:b].set(c)
    out = pl.pallas_call(
        _ada_kernel,
        grid=(depth, n // tn),
        in_specs=[
            pl.BlockSpec((bp, d), lambda i, j: (0, 0)),
            pl.BlockSpec((1, d, tn), lambda i, j: (i, 0, j)),
            pl.BlockSpec((1, 1, tn), lambda i, j: (i, 0, j)),
        ],
        out_specs=pl.BlockSpec((1, bp, tn), lambda i, j: (i, 0, j)),
        out_shape=jax.ShapeDtypeStruct((depth, bp, n), F32),
        compiler_params=_params("parallel", "parallel"),
        name="ada_modulation",
    )(c_pad, ada_w, ada_b.reshape(depth, 1, n))
    return out[:, :b]


def _ffn_kernel(*refs, tf, mixer, final_norm):
    if mixer is None:
        x_ref, shift_ref, scale_ref, gate_ref, ng_ref, wup_ref, wdn_ref, fg_ref = refs[:8]
    else:
        (x_ref, mix_ref, mgate_ref, wout_ref,
         shift_ref, scale_ref, gate_ref, ng_ref, wup_ref, wdn_ref, fg_ref) = refs[:11]
    o_ref, h_ref, acc_ref = refs[-3:]
    hidden = wdn_ref.shape[0]

    x = x_ref[0]
    if mixer == "rows":
        x = x + mgate_ref[0] * _dot(mix_ref[0], wout_ref[...])
    elif mixer == "features":
        parts = [_dot_tn(mix_ref[0, blk], wout_ref[...]) for blk in range(mix_ref.shape[1])]
        x = x + mgate_ref[0] * jnp.concatenate(parts, axis=0)
    o_ref[0] = x
    h_ref[...] = _modulated_norm(x, ng_ref[...], scale_ref[0], shift_ref[0]).astype(BF16)

    for f in range(0, hidden, tf):
        a = _dot(h_ref[...], wup_ref[:, f:f + tf])
        b = _dot(h_ref[...], wup_ref[:, hidden + f:hidden + f + tf])
        down = _dot((_silu(a) * b).astype(BF16), wdn_ref[f:f + tf, :])
        if f == 0:
            acc_ref[...] = down
        else:
            acc_ref[...] += down

    y = o_ref[0] + (0.5 * gate_ref[0]) * acc_ref[...]
    if final_norm:
        y = (y * lax.rsqrt(jnp.mean(y * y, axis=-1, keepdims=True) + EPS)) * fg_ref[...]
    o_ref[0] = y


def _ffn(x, shift, scale, gate, norm_g, w_up, w_down, final_g, final_norm, mixer=None):
    b, s, d = x.shape
    f = w_down.shape[0]
    tf = FFN_CHUNK
    assert f % tf == 0
    tm = ROW_TILE
    operands, specs, mode = [x], [_row_spec(tm, d)], None
    if mixer is not None:
        o, mgate, w_out = mixer
        if o.ndim == 4:
            mode, blk = "features", o.shape[3]
            o_spec = pl.BlockSpec((1, tm // blk, d, blk), lambda bb, i: (bb, i, 0, 0))
        else:
            mode, o_spec = "rows", _row_spec(tm, d)
        operands += [o, mgate, w_out.astype(BF16)]
        specs += [o_spec, _batch_vec_spec(d), _resident((d, d))]
    operands += [shift, scale, gate, norm_g.reshape(1, d), w_up.astype(BF16), w_down.astype(BF16),
                 final_g.reshape(1, d)]
    specs += [_batch_vec_spec(d), _batch_vec_spec(d), _batch_vec_spec(d), _resident((1, d)),
              _resident((d, 2 * f)), _resident((f, d)), _resident((1, d))]
    kern = functools.partial(_ffn_kernel, tf=tf, mixer=mode, final_norm=final_norm)
    return pl.pallas_call(
        kern,
        grid=(b, s // tm),
        in_specs=specs,
        out_specs=_row_spec(tm, d),
        out_shape=jax.ShapeDtypeStruct((b, s, d), F32),
        scratch_shapes=[pltpu.VMEM((tm, d), BF16), pltpu.VMEM((tm, d), F32)],
        compiler_params=_params("parallel", "parallel"),
        name="ffn" + ("" if mode is None else "_" + mode) + ("_final" if final_norm else ""),
    )(*operands)


FOX_SLOT = 2 * FOX_HEAD_DIM
ONES_LANE = N_SPLIT * FOX_HEADS
FOX_VROWS = FOX_HEAD_DIM + 16
LOG2E = 1.4426950408889634


def _fox_proj_kernel(x_ref, shift_ref, scale_ref, ng_ref, wqk_ref, wvt_ref, vones_ref, wgt_ref,
                     wf_ref, bf_ref, tri_ref, place_ref, qa_ref, ka_ref, vt_ref, sgt_ref, carry_ref):
    @pl.when(pl.program_id(1) == 0)
    def _():
        carry_ref[...] = jnp.zeros_like(carry_ref)

    h = _modulated_norm(x_ref[0], ng_ref[...], scale_ref[0], shift_ref[0]).astype(BF16)
    z = _dot(h, wf_ref[...]) + bf_ref[...]
    logf = jnp.minimum(z, 0.0) - jnp.log(1.0 + jnp.exp(-jnp.abs(z)))
    pieces = jnp.concatenate(_split_bf16(logf, N_SPLIT), axis=1)
    part = _dot(tri_ref[...], pieces)
    cum = carry_ref[...] + ((part[:, :LANES] + part[:, LANES:2 * LANES]) + part[:, 2 * LANES:])
    carry_ref[...] = cum[-1:, :]

    lane = lax.broadcasted_iota(jnp.int32, cum.shape, 1)
    operand = jnp.where(lane == ONES_LANE, 1.0, 0.0).astype(F32)
    for j, p in enumerate(_split_bf16(cum * LOG2E, N_SPLIT)):
        pf = p.astype(F32)
        shifted = pf if j == 0 else pltpu.roll(pf, j * FOX_HEADS, axis=1)
        operand = jnp.where((lane >= j * FOX_HEADS) & (lane < (j + 1) * FOX_HEADS), shifted, operand)
    operand = operand.astype(BF16)
    width = qa_ref.shape[2]
    qa_ref[0] = (_dot(h, wqk_ref[:, :width]) + _dot(operand, place_ref[:, :width])).astype(BF16)
    ka_ref[0] = (_dot(h, wqk_ref[:, width:]) + _dot(operand, place_ref[:, width:])).astype(BF16)
    vt_ref[0, 0] = (_dot_nt(wvt_ref[...], h) + vones_ref[...]).astype(BF16)
    sgt_ref[0, 0] = _sigmoid(_dot_nt(wgt_ref[...], h)).astype(BF16)


def _fox_placement():
    place = np.zeros((LANES, 2 * FOX_HEADS * FOX_SLOT), np.float32)
    k_off = FOX_HEADS * FOX_SLOT
    for h in range(FOX_HEADS):
        base = h * FOX_SLOT + FOX_HEAD_DIM
        for j in range(N_SPLIT):
            place[j * FOX_HEADS + h, base + j] = 1.0
            place[ONES_LANE, base + N_SPLIT + j] = 1.0
            place[ONES_LANE, k_off + base + j] = 1.0
            place[j * FOX_HEADS + h, k_off + base + N_SPLIT + j] = -1.0
    return jnp.asarray(place, BF16)


def _fox_proj(x, shift, scale, norm_g, w_in, b_f):
    b, s, d = x.shape
    hds, dh = FOX_HEADS, FOX_HEAD_DIM
    tm = ATT_BLOCK
    att_scale = dh ** -0.5

    def slots(w):
        w = w.reshape(d, hds, dh)
        return jnp.concatenate([w, jnp.zeros_like(w)], axis=-1).reshape(d, hds * FOX_SLOT)

    wqk = jnp.concatenate([slots(w_in[:, 0:d] * (att_scale * LOG2E)), slots(w_in[:, d:2 * d])],
                          axis=1).astype(BF16)
    wv = w_in[:, 2 * d:3 * d].T.reshape(hds, dh, d)
    pad = FOX_VROWS - dh
    wvt = jnp.concatenate([wv, jnp.zeros((hds, pad, d), F32)], axis=1).reshape(hds * FOX_VROWS, d).astype(BF16)
    vones = jnp.concatenate([jnp.zeros((hds, dh, 1), F32), jnp.ones((hds, pad, 1), F32)],
                            axis=1).reshape(hds * FOX_VROWS, 1)
    wgt = w_in[:, 3 * d:4 * d].T.astype(BF16)
    wf = jnp.zeros((d, LANES), F32).at[:, :hds].set(w_in[:, 4 * d:]).astype(BF16)
    bf = jnp.zeros((1, LANES), F32).at[0, :hds].set(b_f)
    tri = jnp.tril(jnp.ones((tm, tm), F32)).astype(BF16)
    qw = hds * FOX_SLOT
    vrows = hds * FOX_VROWS
    return pl.pallas_call(
        _fox_proj_kernel,
        grid=(b, s // tm),
        in_specs=[
            _row_spec(tm, d), _batch_vec_spec(d), _batch_vec_spec(d), _resident((1, d)),
            _resident((d, 2 * qw)), _resident((vrows, d)), _resident((vrows, 1)), _resident((d, d)),
            _resident((d, LANES)), _resident((1, LANES)), _resident((tm, tm)), _resident((LANES, 2 * qw)),
        ],
        out_specs=[_row_spec(tm, qw), _row_spec(tm, qw), _block_major_spec(vrows, tm),
                   _block_major_spec(d, tm)],
        out_shape=[
            jax.ShapeDtypeStruct((b, s, qw), BF16), jax.ShapeDtypeStruct((b, s, qw), BF16),
            jax.ShapeDtypeStruct((b, s // tm, vrows, tm), BF16), jax.ShapeDtypeStruct((b, s // tm, d, tm), BF16),
        ],
        scratch_shapes=[pltpu.VMEM((1, LANES), F32)],
        compiler_params=_params("parallel", "arbitrary"),
        name="fox_proj",
    )(x, shift, scale, norm_g.reshape(1, d), wqk, wvt, vones, wgt, wf, bf, tri, _fox_placement())


def _fox_attn_kernel(qa_ref, ka_ref, vt_ref, sgt_ref, o_ref,
                     acc_ref, m_ref, bias_ref, shift_ref, alpha_ref, s_ref, *, blk, nblk):
    dh = FOX_HEAD_DIM
    heads = range(2)

    kp = lax.broadcasted_iota(jnp.int32, (blk, blk), 0)
    qp = lax.broadcasted_iota(jnp.int32, (blk, blk), 1)
    bias_ref[...] = jnp.where(kp <= qp, 0.0, -jnp.inf)
    acc_ref[...] = jnp.zeros_like(acc_ref)
    m_ref[...] = jnp.full_like(m_ref, -jnp.inf)

    def scores(ij, buf, diagonal):
        i, j = ij
        q = qa_ref[0, pl.ds(pl.multiple_of(i * blk, blk), blk), :]
        k = ka_ref[0, pl.ds(pl.multiple_of(j * blk, blk), blk), :]
        for hh in heads:
            slot = slice(hh * FOX_SLOT, (hh + 1) * FOX_SLOT)
            s = _dot_nt(k[:, slot], q[:, slot])
            if diagonal:
                s = s + bias_ref[...]
            m_old = m_ref[i, hh]
            m_new = jnp.maximum(m_old, jnp.max(s, axis=0, keepdims=True))
            m_ref[i, hh] = m_new
            shift_ref[buf, hh] = m_new
            alpha_ref[buf, hh] = jnp.exp2(m_old - m_new)
            s_ref[buf, hh] = s

    def values(ij, buf, diagonal):
        i, j = ij
        vt = vt_ref[0, j]
        for hh in heads:
            p = jnp.exp2(s_ref[buf, hh] - shift_ref[buf, hh])
            pv = _dot(vt[hh * FOX_VROWS:(hh + 1) * FOX_VROWS, :], p.astype(BF16))
            acc = alpha_ref[buf, hh] * acc_ref[i, hh] + pv
            if diagonal:
                rows = slice(hh * dh, (hh + 1) * dh)
                gated = acc[:dh] * (1.0 / acc[dh:dh + 1]) * sgt_ref[0, i, rows, :].astype(F32)
                o_ref[0, i, rows, :] = gated.astype(BF16)
            else:
                acc_ref[i, hh] = acc

    def sweep(first, advance, n_pairs, diagonal):
        assert n_pairs % 2 == 0 and n_pairs >= 4

        def body(t, carry):
            sc, va = carry
            scores(sc, 1, diagonal)
            values(va, 0, diagonal)
            sc, va = advance(sc), advance(va)
            scores(sc, 0, diagonal)
            values(va, 1, diagonal)
            return advance(sc), advance(va)

        scores(first, 0, diagonal)
        sc, va = lax.fori_loop(0, n_pairs // 2 - 1, body, (advance(first), first))
        scores(sc, 1, diagonal)
        values(va, 0, diagonal)
        values(advance(va), 1, diagonal)

    def next_below_diagonal(ij):
        i, j = ij
        wrap = j + 1 == i
        return jnp.where(wrap, i + 1, i), jnp.where(wrap, 0, j + 1)

    zero, one = jnp.int32(0), jnp.int32(1)
    sweep((one, zero), next_below_diagonal, nblk * (nblk - 1) // 2, False)
    sweep((zero, zero), lambda ij: (ij[0] + 1, ij[1] + 1), nblk, True)


def _fox_attention(qa, ka, vt, sgt):
    b, nblk, d, blk = sgt.shape
    s = nblk * blk
    pairs = FOX_HEADS // 2
    pair_rows = 2 * FOX_HEAD_DIM
    kern = functools.partial(_fox_attn_kernel, blk=blk, nblk=nblk)
    return pl.pallas_call(
        kern,
        grid=(b, pairs),
        in_specs=[
            pl.BlockSpec((1, s, 2 * FOX_SLOT), lambda bb, p: (bb, 0, p)),
            pl.BlockSpec((1, s, 2 * FOX_SLOT), lambda bb, p: (bb, 0, p)),
            pl.BlockSpec((1, nblk, 2 * FOX_VROWS, blk), lambda bb, p: (bb, 0, p, 0)),
            pl.BlockSpec((1, nblk, pair_rows, blk), lambda bb, p: (bb, 0, p, 0)),
        ],
        out_specs=pl.BlockSpec((1, nblk, pair_rows, blk), lambda bb, p: (bb, 0, p, 0)),
        out_shape=jax.ShapeDtypeStruct((b, nblk, d, blk), BF16),
        scratch_shapes=[
            pltpu.VMEM((nblk, 2, FOX_VROWS, blk), F32),
            pltpu.VMEM((nblk, 2, 1, blk), F32),
            pltpu.VMEM((blk, blk), F32),
            pltpu.VMEM((2, 2, 1, blk), F32),
            pltpu.VMEM((2, 2, 1, blk), F32),
            pltpu.VMEM((2, 2, blk, blk), F32),
        ],
        compiler_params=_params("parallel", "parallel"),
        name="fox_attention",
    )(qa, ka, vt, sgt)


def _chunk_prefix_rows(x, chunk):
    row = lax.broadcasted_iota(jnp.int32, x.shape, 0) & (chunk - 1)
    shift = 1
    while shift < chunk:
        x = x + jnp.where(row >= shift, pltpu.roll(x, shift, axis=0), 0.0)
        shift *= 2
    return x


def _hgrn_proj_kernel(x_ref, shift_ref, scale_ref, ng_ref, w_ref, lbl_ref,
                      q_ref, kk_ref, g_ref, v_ref, sg_ref, *, d, layer, chunk):
    lg = lbl_ref[...]
    e = jnp.exp(lg - jnp.max(lg, axis=0, keepdims=True))
    sm = e / jnp.sum(e, axis=0, keepdims=True)
    lb = jnp.sum(sm[0:layer + 1], axis=0, keepdims=True) - sm[0:1]

    h = _modulated_norm(x_ref[0], ng_ref[...], scale_ref[0], shift_ref[0]).astype(BF16)
    proj = _dot(h, w_ref[...])
    f = lb + (1.0 - lb) * _sigmoid(proj[:, d:2 * d])
    q_ref[0] = proj[:, 0:d].astype(BF16)
    kk_ref[0] = (1.0 - f).astype(BF16)
    g_ref[0] = _chunk_prefix_rows(jnp.log(f), chunk) * LOG2E
    v_ref[0] = _silu(proj[:, 2 * d:3 * d]).astype(BF16)
    sg_ref[0] = _silu(proj[:, 3 * d:4 * d]).astype(BF16)


def _hgrn_proj(x, shift, scale, norm_g, w_in, lb_logits, layer):
    b, s, d = x.shape
    depth = lb_logits.shape[0]
    tm = ROW_TILE // 2
    assert tm % HGRN_CHUNK == 0
    kern = functools.partial(_hgrn_proj_kernel, d=d, layer=layer, chunk=HGRN_CHUNK)
    bf = jax.ShapeDtypeStruct((b, s, d), BF16)
    return pl.pallas_call(
        kern,
        grid=(b, s // tm),
        in_specs=[
            _row_spec(tm, d), _batch_vec_spec(d), _batch_vec_spec(d), _resident((1, d)),
            _resident((d, 4 * d)), _resident((depth, d)),
        ],
        out_specs=[_row_spec(tm, d)] * 5,
        out_shape=[bf, bf, jax.ShapeDtypeStruct((b, s, d), F32), bf, bf],
        compiler_params=_params("parallel", "parallel"),
        name="hgrn_proj",
    )(x, shift, scale, norm_g.reshape(1, d), w_in.astype(BF16), lb_logits.astype(F32))


def _pivot_rows(g, half):
    n, lanes = g.shape
    if 2 * half >= SUBLANES:
        g3 = g.reshape(n // (2 * half), 2 * half, lanes)
        piv = jnp.broadcast_to(g3[:, half - 1:half, :], g3.shape)
        return piv.reshape(n, lanes)
    g3 = g.reshape(n // SUBLANES, SUBLANES, lanes)
    sub = lax.broadcasted_iota(jnp.int32, g3.shape, 1)
    out = None
    for grp in range(SUBLANES // (2 * half)):
        r = grp * 2 * half + half - 1
        cand = jnp.broadcast_to(g3[:, r:r + 1, :], g3.shape)
        out = cand if out is None else jnp.where(sub >= grp * 2 * half, cand, out)
    return out.reshape(n, lanes)


def _interleave_rows(lower, upper, half):
    n, lanes = lower.shape
    if half >= SUBLANES:
        lo3 = lower.reshape(n // (2 * half), 2 * half, lanes)
        up3 = upper.reshape(n // (2 * half), 2 * half, lanes)
        return jnp.concatenate([lo3[:, :half], up3[:, half:]], axis=1).reshape(n, lanes)
    row = lax.broadcasted_iota(jnp.int32, lower.shape, 0)
    return jnp.where((row & half) != 0, upper, lower)


def _hgrn_chunk_kernel(q_ref, kk_ref, g_ref, v_ref, sg_ref, gn_ref, o_ref, state_ref, level_ref,
                       *, chunk, heads):
    n_levels = chunk.bit_length() - 1

    @pl.when(pl.program_id(2) == 0)
    def _():
        state_ref[...] = jnp.zeros_like(state_ref)
        t_idx = lax.broadcasted_iota(jnp.int32, (chunk, chunk), 0)
        s_idx = lax.broadcasted_iota(jnp.int32, (chunk, chunk), 1)
        diff = t_idx ^ s_idx
        level = jnp.full((chunk, chunk), -1, jnp.int32)
        for lv in range(n_levels):
            level = level + (diff >= (1 << lv)).astype(jnp.int32)
        level_ref[...] = jnp.where(t_idx > s_idx, level, jnp.where(t_idx == s_idx, n_levels, -1))

    kdim = HGRN_EXPAND
    level = level_ref[...]
    for hh in range(heads):
        cols = slice(hh * kdim, (hh + 1) * kdim)
        qb = q_ref[0, :, cols]
        kb = kk_ref[0, :, cols]
        q = qb.astype(F32)
        kk = kb.astype(F32)
        v = v_ref[0, :, cols]
        g = g_ref[0, :, cols]

        a = jnp.where(level == n_levels, _dot_nt(qb, kb), 0.0)
        for lv in range(n_levels):
            half = 1 << lv
            e = jnp.exp2(-jnp.abs(g - _pivot_rows(g, half)))
            y = (_interleave_rows(kk, q, half) * e).astype(BF16)
            a = jnp.where(level == lv, _dot_nt(y, y), a)

        g_last = g[chunk - 1:chunk, :]
        state_t = state_ref[hh]
        o = _dot_nt((q * jnp.exp2(g)).astype(BF16), state_t.astype(BF16)) + _dot(a.astype(BF16), v)
        k_tail = (kk * jnp.exp2(g_last - g)).astype(BF16)
        state_ref[hh] = state_t * jnp.exp2(g_last) + _dot_tn(v, k_tail)

        o = o * lax.rsqrt(jnp.mean(o * o, axis=-1, keepdims=True) + EPS)
        o = (o * gn_ref[:, cols]) * sg_ref[0, :, cols].astype(F32)
        o_ref[0, :, cols] = o.astype(BF16)


def _hgrn_chunks(q, kk, g, v, sg, g_norm):
    b, s, d = q.shape
    chunk, heads = HGRN_CHUNK, HGRN_HEADS_PER_STEP
    width = heads * HGRN_EXPAND
    spec = pl.BlockSpec((1, chunk, width), lambda bb, hp, c: (bb, c, hp))
    kern = functools.partial(_hgrn_chunk_kernel, chunk=chunk, heads=heads)
    return pl.pallas_call(
        kern,
        grid=(b, d // width, s // chunk),
        in_specs=[spec] * 5 + [pl.BlockSpec((1, width), lambda bb, hp, c: (0, hp))],
        out_specs=spec,
        out_shape=jax.ShapeDtypeStruct((b, s, d), BF16),
        scratch_shapes=[pltpu.VMEM((heads, HGRN_EXPAND, HGRN_EXPAND), F32),
                        pltpu.VMEM((chunk, chunk), jnp.int32)],
        compiler_params=_params("parallel", "parallel", "arbitrary"),
        name="hgrn_chunks",
    )(q, kk, g, v, sg, g_norm.reshape(1, d).astype(F32))


def kernel(x, c, ada_w, ada_b, norm_g, ffn_w_up, ffn_w_down, fox_w_in, fox_b_f, fox_w_out,
           hgrn_w_in, hgrn_norm_g, hgrn_w_out, hgrn_lb_logits, final_norm_g):
    b, s, d = x.shape
    depth = ada_w.shape[0]
    mod = _ada_modulation(c, ada_w, ada_b).reshape(depth, b, N_SUB, 3, 1, d)
    for i in range(depth):
        shift = lambda sub: mod[i, :, sub, 0]
        scale = lambda sub: mod[i, :, sub, 1]
        gate = lambda sub: mod[i, :, sub, 2]
        x = _ffn(x, shift(0), scale(0), gate(0), norm_g[i, 0], ffn_w_up[i, 0], ffn_w_down[i, 0],
                 final_norm_g, False)
        j = i // 2
        if i % 2 == 0:
            qa, ka, vt, sgt = _fox_proj(x, shift(1), scale(1), norm_g[i, 1], fox_w_in[j], fox_b_f[j])
            mixer = (_fox_attention(qa, ka, vt, sgt), gate(1), fox_w_out[j])
        else:
            q, kk, g, v, sg = _hgrn_proj(x, shift(1), scale(1), norm_g[i, 1], hgrn_w_in[j],
                                         hgrn_lb_logits, i)
            mixer = (_hgrn_chunks(q, kk, g, v, sg, hgrn_norm_g[j]), gate(1), hgrn_w_out[j])
        x = _ffn(x, shift(2), scale(2), gate(2), norm_g[i, 2], ffn_w_up[i, 1], ffn_w_down[i, 1],
                 final_norm_g, i == depth - 1, mixer)
    return x
```

```python
import functools

import jax
import jax.numpy as jnp
import numpy as np
from jax import lax
from jax.experimental import pallas as pl
from jax.experimental.pallas import tpu as pltpu

N_SUB = 3
FOX_HEADS = 16
FOX_HEAD_DIM = 64
HGRN_EXPAND = 128
EPS = 1e-6

LANES = 128
SUBLANES = 8
VMEM_LIMIT_BYTES = 56 * 1024 * 1024

ROW_TILE = 1024
FFN_CHUNK = 256
ATT_BLOCK = 512
ATT_UNROLL = 4
HGRN_CHUNK = 128
HGRN_HEADS_PER_STEP = 8
N_SPLIT = 3

BF16 = jnp.bfloat16
F32 = jnp.float32


def _params(*sem):
    return pltpu.CompilerParams(dimension_semantics=sem, vmem_limit_bytes=VMEM_LIMIT_BYTES)


def _resident(shape):
    zeros = (0,) * len(shape)
    return pl.BlockSpec(shape, lambda *_: zeros, pipeline_mode=pl.Buffered(1))


def _row_spec(tm, width):
    return pl.BlockSpec((1, tm, width), lambda b, i: (b, i, 0))


def _block_major_spec(features, tm):
    return pl.BlockSpec((1, 1, features, tm), lambda b, i: (b, i, 0, 0))


def _batch_vec_spec(width):
    return pl.BlockSpec((1, 1, width), lambda b, i: (b, 0, 0))


def _dot(a, b):
    return jnp.dot(a, b, preferred_element_type=F32)


def _dot_nt(a, b):
    return lax.dot_general(a, b, (((1,), (1,)), ((), ())), preferred_element_type=F32)


def _dot_tn(a, b):
    return lax.dot_general(a, b, (((0,), (0,)), ((), ())), preferred_element_type=F32)


def _sigmoid(x):
    return 1.0 / (1.0 + jnp.exp(-x))


def _silu(x):
    return x * _sigmoid(x)


def _modulated_norm(x, norm_g, scale, shift):
    y = x * lax.rsqrt(jnp.mean(x * x, axis=-1, keepdims=True) + EPS)
    return (y * norm_g) * (1.0 + scale) + shift


def _split_bf16(x, n):
    pieces = []
    rest = x
    for _ in range(n):
        p = rest.astype(BF16)
        pieces.append(p)
        rest = rest - p.astype(F32)
    return pieces


def _ada_kernel(c_ref, w_ref, b_ref, o_ref):
    cond = _silu(c_ref[...])
    y = jnp.dot(cond, w_ref[0], preferred_element_type=F32,
                precision=lax.Precision.HIGHEST)
    o_ref[0] = y + b_ref[0]


def _ada_modulation(c, ada_w, ada_b):
    depth, d, n = ada_w.shape
    b = c.shape[0]
    bp = -(-b // SUBLANES) * SUBLANES
    tn = n // 9 if n % (9 * LANES) == 0 else n
    c_pad = jnp.zeros((bp, d), F32).at[:b].set(c)
    out = pl.pallas_call(
        _ada_kernel,
        grid=(depth, n // tn),
        in_specs=[
            pl.BlockSpec((bp, d), lambda i, j: (0, 0)),
            pl.BlockSpec((1, d, tn), lambda i, j: (i, 0, j)),
            pl.BlockSpec((1, 1, tn), lambda i, j: (i, 0, j)),
        ],
        out_specs=pl.BlockSpec((1, bp, tn), lambda i, j: (i, 0, j)),
        out_shape=jax.ShapeDtypeStruct((depth, bp, n), F32),
        compiler_params=_params("parallel", "parallel"),
        name="ada_modulation",
    )(c_pad, ada_w, ada_b.reshape(depth, 1, n))
    return out[:, :b]


def _ffn_kernel(*refs, tf, mixer, final_norm):
    if mixer is None:
        x_ref, shift_ref, scale_ref, gate_ref, ng_ref, wup_ref, wdn_ref, fg_ref = refs[:8]
    else:
        (x_ref, mix_ref, mgate_ref, wout_ref,
         shift_ref, scale_ref, gate_ref, ng_ref, wup_ref, wdn_ref, fg_ref) = refs[:11]
    o_ref, h_ref, acc_ref = refs[-3:]
    wup_ref, wdn_ref = wup_ref.at[0, 0], wdn_ref.at[0, 0]
    hidden = wdn_ref.shape[0]

    x = x_ref[0]
    if mixer == "rows":
        x = x + mgate_ref[0] * _dot(mix_ref[0], wout_ref[...])
    elif mixer == "features":
        parts = [_dot_tn(mix_ref[0, blk], wout_ref[...]) for blk in range(mix_ref.shape[1])]
        x = x + mgate_ref[0] * jnp.concatenate(parts, axis=0)
    o_ref[0] = x
    h_ref[...] = _modulated_norm(x, ng_ref[...], scale_ref[0], shift_ref[0]).astype(BF16)

    for f in range(0, hidden, tf):
        a = _dot(h_ref[...], wup_ref[:, f:f + tf])
        b = _dot(h_ref[...], wup_ref[:, hidden + f:hidden + f + tf])
        down = _dot((_silu(a) * b).astype(BF16), wdn_ref[f:f + tf, :])
        if f == 0:
            acc_ref[...] = down
        else:
            acc_ref[...] += down

    y = o_ref[0] + (0.5 * gate_ref[0]) * acc_ref[...]
    if final_norm:
        y = (y * lax.rsqrt(jnp.mean(y * y, axis=-1, keepdims=True) + EPS)) * fg_ref[...]
    o_ref[0] = y


def _ffn(x, shift, scale, gate, norm_g, w_up, w_down, which, final_g, final_norm, mixer=None):
    b, s, d = x.shape
    f = w_down.shape[2]
    stack_index = tuple(which) + (0, 0)
    stack_spec = lambda rows, cols: pl.BlockSpec((1, 1, rows, cols), lambda bb, i: stack_index,
                                                 pipeline_mode=pl.Buffered(1))
    tf = FFN_CHUNK
    assert f % tf == 0
    tm = ROW_TILE
    operands, specs, mode = [x], [_row_spec(tm, d)], None
    if mixer is not None:
        o, mgate, w_out = mixer
        if o.ndim == 4:
            mode, blk = "features", o.shape[3]
            o_spec = pl.BlockSpec((1, tm // blk, d, blk), lambda bb, i: (bb, i, 0, 0))
        else:
            mode, o_spec = "rows", _row_spec(tm, d)
        operands += [o, mgate, w_out.astype(BF16)]
        specs += [o_spec, _batch_vec_spec(d), _resident((d, d))]
    operands += [shift, scale, gate, norm_g.reshape(1, d), w_up, w_down, final_g.reshape(1, d)]
    specs += [_batch_vec_spec(d), _batch_vec_spec(d), _batch_vec_spec(d), _resident((1, d)),
              stack_spec(d, 2 * f), stack_spec(f, d), _resident((1, d))]
    kern = functools.partial(_ffn_kernel, tf=tf, mixer=mode, final_norm=final_norm)
    return pl.pallas_call(
        kern,
        grid=(b, s // tm),
        in_specs=specs,
        out_specs=_row_spec(tm, d),
        out_shape=jax.ShapeDtypeStruct((b, s, d), F32),
        scratch_shapes=[pltpu.VMEM((tm, d), BF16), pltpu.VMEM((tm, d), F32)],
        compiler_params=_params("parallel", "parallel"),
        name="ffn" + ("" if mode is None else "_" + mode) + ("_final" if final_norm else ""),
    )(*operands)


FOX_SLOT = 2 * FOX_HEAD_DIM
ONES_LANE = N_SPLIT * FOX_HEADS
FOX_VROWS = FOX_HEAD_DIM + 16
LOG2E = 1.4426950408889634


def _fox_proj_kernel(x_ref, shift_ref, scale_ref, ng_ref, wqk_ref, wvt_ref, vones_ref, wgt_ref,
                     wf_ref, bf_ref, tri_ref, place_ref, qa_ref, ka_ref, vt_ref, sgt_ref, carry_ref):
    @pl.when(pl.program_id(1) == 0)
    def _():
        carry_ref[...] = jnp.zeros_like(carry_ref)

    h = _modulated_norm(x_ref[0], ng_ref[...], scale_ref[0], shift_ref[0]).astype(BF16)
    z = _dot(h, wf_ref[...]) + bf_ref[...]
    logf = jnp.minimum(z, 0.0) - jnp.log(1.0 + jnp.exp(-jnp.abs(z)))
    pieces = jnp.concatenate(_split_bf16(logf, N_SPLIT), axis=1)
    part = _dot(tri_ref[...], pieces)
    cum = carry_ref[...] + ((part[:, :LANES] + part[:, LANES:2 * LANES]) + part[:, 2 * LANES:])
    carry_ref[...] = cum[-1:, :]

    lane = lax.broadcasted_iota(jnp.int32, cum.shape, 1)
    operand = jnp.where(lane == ONES_LANE, 1.0, 0.0).astype(F32)
    for j, p in enumerate(_split_bf16(cum * LOG2E, N_SPLIT)):
        pf = p.astype(F32)
        shifted = pf if j == 0 else pltpu.roll(pf, j * FOX_HEADS, axis=1)
        operand = jnp.where((lane >= j * FOX_HEADS) & (lane < (j + 1) * FOX_HEADS), shifted, operand)
    operand = operand.astype(BF16)
    width = qa_ref.shape[2]
    qa_ref[0] = (_dot(h, wqk_ref[:, :width]) + _dot(operand, place_ref[:, :width])).astype(BF16)
    ka_ref[0] = (_dot(h, wqk_ref[:, width:]) + _dot(operand, place_ref[:, width:])).astype(BF16)
    vt_ref[0, 0] = (_dot_nt(wvt_ref[...], h) + vones_ref[...]).astype(BF16)
    sgt_ref[0, 0] = _sigmoid(_dot_nt(wgt_ref[...], h)).astype(BF16)


def _fox_placement():
    place = np.zeros((LANES, 2 * FOX_HEADS * FOX_SLOT), np.float32)
    k_off = FOX_HEADS * FOX_SLOT
    for h in range(FOX_HEADS):
        base = h * FOX_SLOT + FOX_HEAD_DIM
        for j in range(N_SPLIT):
            place[j * FOX_HEADS + h, base + j] = 1.0
            place[ONES_LANE, base + N_SPLIT + j] = 1.0
            place[ONES_LANE, k_off + base + j] = 1.0
            place[j * FOX_HEADS + h, k_off + base + N_SPLIT + j] = -1.0
    return jnp.asarray(place, BF16)


def _fox_proj(x, shift, scale, norm_g, w_in, b_f):
    b, s, d = x.shape
    hds, dh = FOX_HEADS, FOX_HEAD_DIM
    tm = ATT_BLOCK
    att_scale = dh ** -0.5

    def slots(w):
        w = w.reshape(d, hds, dh)
        return jnp.concatenate([w, jnp.zeros_like(w)], axis=-1).reshape(d, hds * FOX_SLOT)

    wqk = jnp.concatenate([slots(w_in[:, 0:d] * (att_scale * LOG2E)), slots(w_in[:, d:2 * d])],
                          axis=1).astype(BF16)
    wv = w_in[:, 2 * d:3 * d].T.reshape(hds, dh, d)
    pad = FOX_VROWS - dh
    wvt = jnp.concatenate([wv, jnp.zeros((hds, pad, d), F32)], axis=1).reshape(hds * FOX_VROWS, d).astype(BF16)
    vones = jnp.concatenate([jnp.zeros((hds, dh, 1), F32), jnp.ones((hds, pad, 1), F32)],
                            axis=1).reshape(hds * FOX_VROWS, 1)
    wgt = w_in[:, 3 * d:4 * d].T.astype(BF16)
    wf = jnp.zeros((d, LANES), F32).at[:, :hds].set(w_in[:, 4 * d:]).astype(BF16)
    bf = jnp.zeros((1, LANES), F32).at[0, :hds].set(b_f)
    tri = jnp.tril(jnp.ones((tm, tm), F32)).astype(BF16)
    qw = hds * FOX_SLOT
    vrows = hds * FOX_VROWS
    return pl.pallas_call(
        _fox_proj_kernel,
        grid=(b, s // tm),
        in_specs=[
            _row_spec(tm, d), _batch_vec_spec(d), _batch_vec_spec(d), _resident((1, d)),
            _resident((d, 2 * qw)), _resident((vrows, d)), _resident((vrows, 1)), _resident((d, d)),
            _resident((d, LANES)), _resident((1, LANES)), _resident((tm, tm)), _resident((LANES, 2 * qw)),
        ],
        out_specs=[_row_spec(tm, qw), _row_spec(tm, qw), _block_major_spec(vrows, tm),
                   _block_major_spec(d, tm)],
        out_shape=[
            jax.ShapeDtypeStruct((b, s, qw), BF16), jax.ShapeDtypeStruct((b, s, qw), BF16),
            jax.ShapeDtypeStruct((b, s // tm, vrows, tm), BF16), jax.ShapeDtypeStruct((b, s // tm, d, tm), BF16),
        ],
        scratch_shapes=[pltpu.VMEM((1, LANES), F32)],
        compiler_params=_params("parallel", "arbitrary"),
        name="fox_proj",
    )(x, shift, scale, norm_g.reshape(1, d), wqk, wvt, vones, wgt, wf, bf, tri, _fox_placement())


def _fox_attn_kernel(qa_ref, ka_ref, vt_ref, sgt_ref, o_ref,
                     acc_ref, m_ref, bias_ref, shift_ref, alpha_ref, s_ref, *, blk, nblk):
    dh = FOX_HEAD_DIM
    heads = range(2)

    kp = lax.broadcasted_iota(jnp.int32, (blk, blk), 0)
    qp = lax.broadcasted_iota(jnp.int32, (blk, blk), 1)
    bias_ref[...] = jnp.where(kp <= qp, 0.0, -jnp.inf)
    acc_ref[...] = jnp.zeros_like(acc_ref)
    m_ref[...] = jnp.full_like(m_ref, -jnp.inf)

    def scores(ij, buf, diagonal):
        i, j = ij
        q = qa_ref[0, pl.ds(pl.multiple_of(i * blk, blk), blk), :]
        k = ka_ref[0, pl.ds(pl.multiple_of(j * blk, blk), blk), :]
        for hh in heads:
            slot = slice(hh * FOX_SLOT, (hh + 1) * FOX_SLOT)
            s = _dot_nt(k[:, slot], q[:, slot])
            if diagonal:
                s = s + bias_ref[...]
            m_old = m_ref[i, hh]
            m_new = jnp.maximum(m_old, jnp.max(s, axis=0, keepdims=True))
            m_ref[i, hh] = m_new
            shift_ref[buf, hh] = m_new
            alpha_ref[buf, hh] = jnp.exp2(m_old - m_new)
            s_ref[buf, hh] = s

    def values(ij, buf, diagonal):
        i, j = ij
        vt = vt_ref[0, j]
        for hh in heads:
            p = jnp.exp2(s_ref[buf, hh] - shift_ref[buf, hh])
            pv = _dot(vt[hh * FOX_VROWS:(hh + 1) * FOX_VROWS, :], p.astype(BF16))
            acc = alpha_ref[buf, hh] * acc_ref[i, hh] + pv
            if diagonal:
                rows = slice(hh * dh, (hh + 1) * dh)
                gated = acc[:dh] * (1.0 / acc[dh:dh + 1]) * sgt_ref[0, i, rows, :].astype(F32)
                o_ref[0, i, rows, :] = gated.astype(BF16)
            else:
                acc_ref[i, hh] = acc

    def sweep(first, advance, n_pairs, diagonal, unroll):
        assert unroll % 2 == 0 and n_pairs > unroll

        def steps(carry, count):
            sc, va = carry
            for r in range(count):
                scores(sc, (r + 1) % 2, diagonal)
                values(va, r % 2, diagonal)
                sc, va = advance(sc), advance(va)
            return sc, va

        scores(first, 0, diagonal)
        carry = lax.fori_loop(0, (n_pairs - 1) // unroll, lambda t, c: steps(c, unroll),
                              (advance(first), first))
        tail = (n_pairs - 1) % unroll
        _, va = steps(carry, tail)
        values(va, tail % 2, diagonal)

    def next_below_diagonal(ij):
        i, j = ij
        wrap = j + 1 == i
        return jnp.where(wrap, i + 1, i), jnp.where(wrap, 0, j + 1)

    zero, one = jnp.int32(0), jnp.int32(1)
    sweep((one, zero), next_below_diagonal, nblk * (nblk - 1) // 2, False, ATT_UNROLL)
    sweep((zero, zero), lambda ij: (ij[0] + 1, ij[1] + 1), nblk, True, ATT_UNROLL)


def _fox_attention(qa, ka, vt, sgt):
    b, nblk, d, blk = sgt.shape
    s = nblk * blk
    pairs = FOX_HEADS // 2
    pair_rows = 2 * FOX_HEAD_DIM
    kern = functools.partial(_fox_attn_kernel, blk=blk, nblk=nblk)
    return pl.pallas_call(
        kern,
        grid=(b, pairs),
        in_specs=[
            pl.BlockSpec((1, s, 2 * FOX_SLOT), lambda bb, p: (bb, 0, p)),
            pl.BlockSpec((1, s, 2 * FOX_SLOT), lambda bb, p: (bb, 0, p)),
            pl.BlockSpec((1, nblk, 2 * FOX_VROWS, blk), lambda bb, p: (bb, 0, p, 0)),
            pl.BlockSpec((1, nblk, pair_rows, blk), lambda bb, p: (bb, 0, p, 0)),
        ],
        out_specs=pl.BlockSpec((1, nblk, pair_rows, blk), lambda bb, p: (bb, 0, p, 0)),
        out_shape=jax.ShapeDtypeStruct((b, nblk, d, blk), BF16),
        scratch_shapes=[
            pltpu.VMEM((nblk, 2, FOX_VROWS, blk), F32),
            pltpu.VMEM((nblk, 2, 1, blk), F32),
            pltpu.VMEM((blk, blk), F32),
            pltpu.VMEM((2, 2, 1, blk), F32),
            pltpu.VMEM((2, 2, 1, blk), F32),
            pltpu.VMEM((2, 2, blk, blk), F32),
        ],
        compiler_params=_params("parallel", "parallel"),
        name="fox_attention",
    )(qa, ka, vt, sgt)


def _chunk_prefix_rows(x, chunk):
    row = lax.broadcasted_iota(jnp.int32, x.shape, 0) & (chunk - 1)
    shift = 1
    while shift < chunk:
        x = x + jnp.where(row >= shift, pltpu.roll(x, shift, axis=0), 0.0)
        shift *= 2
    return x


def _hgrn_proj_kernel(x_ref, shift_ref, scale_ref, ng_ref, w_ref, lbl_ref,
                      q_ref, kk_ref, g_ref, v_ref, sg_ref, *, d, layer, chunk):
    lg = lbl_ref[...]
    e = jnp.exp(lg - jnp.max(lg, axis=0, keepdims=True))
    sm = e / jnp.sum(e, axis=0, keepdims=True)
    lb = jnp.sum(sm[0:layer + 1], axis=0, keepdims=True) - sm[0:1]

    h = _modulated_norm(x_ref[0], ng_ref[...], scale_ref[0], shift_ref[0]).astype(BF16)
    q_ref[0] = _dot(h, w_ref[:, 0:d]).astype(BF16)
    f = lb + (1.0 - lb) * _sigmoid(_dot(h, w_ref[:, d:2 * d]))
    kk_ref[0] = (1.0 - f).astype(BF16)
    g_ref[0] = _chunk_prefix_rows(jnp.log(f), chunk) * LOG2E
    v_ref[0] = _silu(_dot(h, w_ref[:, 2 * d:3 * d])).astype(BF16)
    sg_ref[0] = _silu(_dot(h, w_ref[:, 3 * d:4 * d])).astype(BF16)


def _hgrn_proj(x, shift, scale, norm_g, w_in, lb_logits, layer):
    b, s, d = x.shape
    depth = lb_logits.shape[0]
    tm = ROW_TILE // 2
    assert tm % HGRN_CHUNK == 0
    kern = functools.partial(_hgrn_proj_kernel, d=d, layer=layer, chunk=HGRN_CHUNK)
    bf = jax.ShapeDtypeStruct((b, s, d), BF16)
    return pl.pallas_call(
        kern,
        grid=(b, s // tm),
        in_specs=[
            _row_spec(tm, d), _batch_vec_spec(d), _batch_vec_spec(d), _resident((1, d)),
            _resident((d, 4 * d)), _resident((depth, d)),
        ],
        out_specs=[_row_spec(tm, d)] * 5,
        out_shape=[bf, bf, jax.ShapeDtypeStruct((b, s, d), F32), bf, bf],
        compiler_params=_params("parallel", "parallel"),
        name="hgrn_proj",
    )(x, shift, scale, norm_g.reshape(1, d), w_in.astype(BF16), lb_logits.astype(F32))


def _pivot_rows(g, half):
    n, lanes = g.shape
    if 2 * half >= SUBLANES:
        g3 = g.reshape(n // (2 * half), 2 * half, lanes)
        piv = jnp.broadcast_to(g3[:, half - 1:half, :], g3.shape)
        return piv.reshape(n, lanes)
    g3 = g.reshape(n // SUBLANES, SUBLANES, lanes)
    sub = lax.broadcasted_iota(jnp.int32, g3.shape, 1)
    out = None
    for grp in range(SUBLANES // (2 * half)):
        r = grp * 2 * half + half - 1
        cand = jnp.broadcast_to(g3[:, r:r + 1, :], g3.shape)
        out = cand if out is None else jnp.where(sub >= grp * 2 * half, cand, out)
    return out.reshape(n, lanes)


def _interleave_rows(lower, upper, half):
    n, lanes = lower.shape
    if half >= SUBLANES:
        lo3 = lower.reshape(n // (2 * half), 2 * half, lanes)
        up3 = upper.reshape(n // (2 * half), 2 * half, lanes)
        return jnp.concatenate([lo3[:, :half], up3[:, half:]], axis=1).reshape(n, lanes)
    row = lax.broadcasted_iota(jnp.int32, lower.shape, 0)
    return jnp.where((row & half) != 0, upper, lower)


def _hgrn_chunk_kernel(q_ref, kk_ref, g_ref, v_ref, sg_ref, gn_ref, o_ref, state_ref, level_ref,
                       *, chunk, heads):
    n_levels = chunk.bit_length() - 1

    @pl.when(pl.program_id(2) == 0)
    def _():
        state_ref[...] = jnp.zeros_like(state_ref)
        t_idx = lax.broadcasted_iota(jnp.int32, (chunk, chunk), 0)
        s_idx = lax.broadcasted_iota(jnp.int32, (chunk, chunk), 1)
        diff = t_idx ^ s_idx
        level = jnp.full((chunk, chunk), -1, jnp.int32)
        for lv in range(n_levels):
            level = level + (diff >= (1 << lv)).astype(jnp.int32)
        level_ref[...] = jnp.where(t_idx > s_idx, level, jnp.where(t_idx == s_idx, n_levels, -1))

    kdim = HGRN_EXPAND
    level = level_ref[...]
    for hh in range(heads):
        cols = slice(hh * kdim, (hh + 1) * kdim)
        qb = q_ref[0, :, cols]
        kb = kk_ref[0, :, cols]
        q = qb.astype(F32)
        kk = kb.astype(F32)
        v = v_ref[0, :, cols]
        g = g_ref[0, :, cols]

        a = jnp.where(level == n_levels, _dot_nt(qb, kb), 0.0)
        for lv in range(n_levels):
            half = 1 << lv
            e = jnp.exp2(-jnp.abs(g - _pivot_rows(g, half)))
            y = (_interleave_rows(kk, q, half) * e).astype(BF16)
            a = jnp.where(level == lv, _dot_nt(y, y), a)

        g_last = g[chunk - 1:chunk, :]
        state_t = state_ref[hh]
        o = _dot_nt((q * jnp.exp2(g)).astype(BF16), state_t.astype(BF16)) + _dot(a.astype(BF16), v)
        k_tail = (kk * jnp.exp2(g_last - g)).astype(BF16)
        state_ref[hh] = state_t * jnp.exp2(g_last) + _dot_tn(v, k_tail)

        o = o * lax.rsqrt(jnp.mean(o * o, axis=-1, keepdims=True) + EPS)
        o = (o * gn_ref[:, cols]) * sg_ref[0, :, cols].astype(F32)
        o_ref[0, :, cols] = o.astype(BF16)


def _hgrn_chunks(q, kk, g, v, sg, g_norm):
    b, s, d = q.shape
    chunk, heads = HGRN_CHUNK, HGRN_HEADS_PER_STEP
    width = heads * HGRN_EXPAND
    spec = pl.BlockSpec((1, chunk, width), lambda bb, hp, c: (bb, c, hp))
    kern = functools.partial(_hgrn_chunk_kernel, chunk=chunk, heads=heads)
    return pl.pallas_call(
        kern,
        grid=(b, d // width, s // chunk),
        in_specs=[spec] * 5 + [pl.BlockSpec((1, width), lambda bb, hp, c: (0, hp))],
        out_specs=spec,
        out_shape=jax.ShapeDtypeStruct((b, s, d), BF16),
        scratch_shapes=[pltpu.VMEM((heads, HGRN_EXPAND, HGRN_EXPAND), F32),
                        pltpu.VMEM((chunk, chunk), jnp.int32)],
        compiler_params=_params("parallel", "parallel", "arbitrary"),
        name="hgrn_chunks",
    )(q, kk, g, v, sg, g_norm.reshape(1, d).astype(F32))


def kernel(x, c, ada_w, ada_b, norm_g, ffn_w_up, ffn_w_down, fox_w_in, fox_b_f, fox_w_out,
           hgrn_w_in, hgrn_norm_g, hgrn_w_out, hgrn_lb_logits, final_norm_g):
    b, s, d = x.shape
    depth = ada_w.shape[0]
    mod = _ada_modulation(c, ada_w, ada_b).reshape(depth, b, N_SUB, 3, 1, d)
    w_up, w_down = ffn_w_up.astype(BF16), ffn_w_down.astype(BF16)
    for i in range(depth):
        shift = lambda sub: mod[i, :, sub, 0]
        scale = lambda sub: mod[i, :, sub, 1]
        gate = lambda sub: mod[i, :, sub, 2]
        x = _ffn(x, shift(0), scale(0), gate(0), norm_g[i, 0], w_up, w_down, (i, 0),
                 final_norm_g, False)
        j = i // 2
        if i % 2 == 0:
            qa, ka, vt, sgt = _fox_proj(x, shift(1), scale(1), norm_g[i, 1], fox_w_in[j], fox_b_f[j])
            mixer = (_fox_attention(qa, ka, vt, sgt), gate(1), fox_w_out[j])
        else:
            q, kk, g, v, sg = _hgrn_proj(x, shift(1), scale(1), norm_g[i, 1], hgrn_w_in[j],
                                         hgrn_lb_logits, i)
            mixer = (_hgrn_chunks(q, kk, g, v, sg, hgrn_norm_g[j]), gate(1), hgrn_w_out[j])
        x = _ffn(x, shift(2), scale(2), gate(2), norm_g[i, 2], w_up, w_down, (i, 1),
                 final_norm_g, i == depth - 1, mixer)
    return x
```

```python
import functools

import jax
import jax.numpy as jnp
import numpy as np
from jax import lax
from jax.experimental import pallas as pl
from jax.experimental.pallas import tpu as pltpu

N_SUB = 3
FOX_HEADS = 16
FOX_HEAD_DIM = 64
HGRN_EXPAND = 128
EPS = 1e-6

LANES = 128
SUBLANES = 8
VMEM_LIMIT_BYTES = 56 * 1024 * 1024

ROW_TILE = 1024
FFN_CHUNK = 256
ATT_BLOCK = 512
ATT_UNROLL = 4
HGRN_CHUNK = 128
HGRN_HEADS_PER_STEP = 8
N_SPLIT = 3

BF16 = jnp.bfloat16
F32 = jnp.float32


def _params(*sem):
    return pltpu.CompilerParams(dimension_semantics=sem, vmem_limit_bytes=VMEM_LIMIT_BYTES)


def _resident(shape):
    zeros = (0,) * len(shape)
    return pl.BlockSpec(shape, lambda *_: zeros, pipeline_mode=pl.Buffered(1))


def _row_spec(tm, width):
    return pl.BlockSpec((1, tm, width), lambda b, i: (b, i, 0))


def _block_major_spec(features, tm):
    return pl.BlockSpec((1, 1, features, tm), lambda b, i: (b, i, 0, 0))


def _batch_vec_spec(width):
    return pl.BlockSpec((1, 1, width), lambda b, i: (b, 0, 0))


def _dot(a, b):
    return jnp.dot(a, b, preferred_element_type=F32)


def _dot_nt(a, b):
    return lax.dot_general(a, b, (((1,), (1,)), ((), ())), preferred_element_type=F32)


def _dot_tn(a, b):
    return lax.dot_general(a, b, (((0,), (0,)), ((), ())), preferred_element_type=F32)


def _sigmoid(x):
    return 1.0 / (1.0 + jnp.exp(-x))


def _silu(x):
    return x * _sigmoid(x)


def _modulated_norm(x, norm_g, scale, shift):
    y = x * lax.rsqrt(jnp.mean(x * x, axis=-1, keepdims=True) + EPS)
    return (y * norm_g) * (1.0 + scale) + shift


def _split_bf16(x, n):
    pieces = []
    rest = x
    for _ in range(n):
        p = rest.astype(BF16)
        pieces.append(p)
        rest = rest - p.astype(F32)
    return pieces


def _ada_kernel(c_ref, w_ref, b_ref, o_ref):
    cond = _silu(c_ref[...])
    y = jnp.dot(cond, w_ref[0], preferred_element_type=F32,
                precision=lax.Precision.HIGHEST)
    o_ref[0] = y + b_ref[0]


def _ada_modulation(c, ada_w, ada_b):
    depth, d, n = ada_w.shape
    b = c.shape[0]
    bp = -(-b // SUBLANES) * SUBLANES
    tn = n // 9 if n % (9 * LANES) == 0 else n
    c_pad = jnp.zeros((bp, d), F32).at[:b].set(c)
    out = pl.pallas_call(
        _ada_kernel,
        grid=(depth, n // tn),
        in_specs=[
            pl.BlockSpec((bp, d), lambda i, j: (0, 0)),
            pl.BlockSpec((1, d, tn), lambda i, j: (i, 0, j)),
            pl.BlockSpec((1, 1, tn), lambda i, j: (i, 0, j)),
        ],
        out_specs=pl.BlockSpec((1, bp, tn), lambda i, j: (i, 0, j)),
        out_shape=jax.ShapeDtypeStruct((depth, bp, n), F32),
        compiler_params=_params("parallel", "parallel"),
        name="ada_modulation",
    )(c_pad, ada_w, ada_b.reshape(depth, 1, n))
    return out[:, :b]


def _ffn_kernel(*refs, tf, mixer, final_norm):
    if mixer is None:
        x_ref, shift_ref, scale_ref, gate_ref, ng_ref, wup_ref, wdn_ref, fg_ref = refs[:8]
    else:
        (x_ref, mix_ref, mgate_ref, wout_ref,
         shift_ref, scale_ref, gate_ref, ng_ref, wup_ref, wdn_ref, fg_ref) = refs[:11]
    o_ref, h_ref, acc_ref = refs[-3:]
    wup_ref, wdn_ref = wup_ref.at[0, 0], wdn_ref.at[0, 0]
    hidden = wdn_ref.shape[0]

    x = x_ref[0]
    if mixer == "rows":
        x = x + mgate_ref[0] * _dot(mix_ref[0], wout_ref[...])
    elif mixer == "features":
        parts = [_dot_tn(mix_ref[0, blk], wout_ref[...]) for blk in range(mix_ref.shape[1])]
        x = x + mgate_ref[0] * jnp.concatenate(parts, axis=0)
    o_ref[0] = x
    h_ref[...] = _modulated_norm(x, ng_ref[...], scale_ref[0], shift_ref[0]).astype(BF16)

    for f in range(0, hidden, tf):
        a = _dot(h_ref[...], wup_ref[:, f:f + tf])
        b = _dot(h_ref[...], wup_ref[:, hidden + f:hidden + f + tf])
        down = _dot((_silu(a) * b).astype(BF16), wdn_ref[f:f + tf, :])
        if f == 0:
            acc_ref[...] = down
        else:
            acc_ref[...] += down

    y = o_ref[0] + (0.5 * gate_ref[0]) * acc_ref[...]
    if final_norm:
        y = (y * lax.rsqrt(jnp.mean(y * y, axis=-1, keepdims=True) + EPS)) * fg_ref[...]
    o_ref[0] = y


def _ffn(x, shift, scale, gate, norm_g, w_up, w_down, which, final_g, final_norm, mixer=None):
    b, s, d = x.shape
    f = w_down.shape[2]
    stack_index = tuple(which) + (0, 0)
    stack_spec = lambda rows, cols: pl.BlockSpec((1, 1, rows, cols), lambda bb, i: stack_index,
                                                 pipeline_mode=pl.Buffered(1))
    tf = FFN_CHUNK
    assert f % tf == 0
    tm = ROW_TILE
    operands, specs, mode = [x], [_row_spec(tm, d)], None
    if mixer is not None:
        o, mgate, w_out = mixer
        if o.ndim == 4:
            mode, blk = "features", o.shape[3]
            o_spec = pl.BlockSpec((1, tm // blk, d, blk), lambda bb, i: (bb, i, 0, 0))
        else:
            mode, o_spec = "rows", _row_spec(tm, d)
        operands += [o, mgate, w_out.astype(BF16)]
        specs += [o_spec, _batch_vec_spec(d), _resident((d, d))]
    operands += [shift, scale, gate, norm_g.reshape(1, d), w_up, w_down, final_g.reshape(1, d)]
    specs += [_batch_vec_spec(d), _batch_vec_spec(d), _batch_vec_spec(d), _resident((1, d)),
              stack_spec(d, 2 * f), stack_spec(f, d), _resident((1, d))]
    kern = functools.partial(_ffn_kernel, tf=tf, mixer=mode, final_norm=final_norm)
    return pl.pallas_call(
        kern,
        grid=(b, s // tm),
        in_specs=specs,
        out_specs=_row_spec(tm, d),
        out_shape=jax.ShapeDtypeStruct((b, s, d), F32),
        scratch_shapes=[pltpu.VMEM((tm, d), BF16), pltpu.VMEM((tm, d), F32)],
        compiler_params=_params("parallel", "parallel"),
        name="ffn" + ("" if mode is None else "_" + mode) + ("_final" if final_norm else ""),
    )(*operands)


FOX_SLOT = 2 * FOX_HEAD_DIM
ONES_LANE = N_SPLIT * FOX_HEADS
FOX_VROWS = FOX_HEAD_DIM + 16
LOG2E = 1.4426950408889634


def _fox_proj_kernel(x_ref, shift_ref, scale_ref, ng_ref, wqk_ref, wvt_ref, vones_ref, wgt_ref,
                     wf_ref, bf_ref, tri_ref, place_ref, qa_ref, ka_ref, vt_ref, sgt_ref, carry_ref):
    @pl.when(pl.program_id(1) == 0)
    def _():
        carry_ref[...] = jnp.zeros_like(carry_ref)

    h = _modulated_norm(x_ref[0], ng_ref[...], scale_ref[0], shift_ref[0]).astype(BF16)
    z = _dot(h, wf_ref[...]) + bf_ref[...]
    logf = jnp.minimum(z, 0.0) - jnp.log(1.0 + jnp.exp(-jnp.abs(z)))
    pieces = jnp.concatenate(_split_bf16(logf, N_SPLIT), axis=1)
    part = _dot(tri_ref[...], pieces)
    cum = carry_ref[...] + ((part[:, :LANES] + part[:, LANES:2 * LANES]) + part[:, 2 * LANES:])
    carry_ref[...] = cum[-1:, :]

    lane = lax.broadcasted_iota(jnp.int32, cum.shape, 1)
    operand = jnp.where(lane == ONES_LANE, 1.0, 0.0).astype(F32)
    for j, p in enumerate(_split_bf16(cum * LOG2E, N_SPLIT)):
        pf = p.astype(F32)
        shifted = pf if j == 0 else pltpu.roll(pf, j * FOX_HEADS, axis=1)
        operand = jnp.where((lane >= j * FOX_HEADS) & (lane < (j + 1) * FOX_HEADS), shifted, operand)
    operand = operand.astype(BF16)
    d = wqk_ref.shape[0]
    slot_lane = lax.broadcasted_iota(jnp.int32, (operand.shape[0], FOX_SLOT), 1)
    for out_ref, w_cols, place_cols in ((qa_ref, 0, 0), (ka_ref, d, FOX_HEADS * FOX_SLOT)):
        packed = _dot(h, wqk_ref[:, w_cols:w_cols + d])
        decay = _dot(operand, place_ref[:, place_cols:place_cols + FOX_HEADS * FOX_SLOT])
        for hd in range(FOX_HEADS):
            pair = packed[:, (hd // 2) * FOX_SLOT:(hd // 2 + 1) * FOX_SLOT]
            if hd % 2:
                pair = pltpu.roll(pair, FOX_HEAD_DIM, axis=1)
            cols = slice(hd * FOX_SLOT, (hd + 1) * FOX_SLOT)
            out_ref[0, :, cols] = jnp.where(slot_lane < FOX_HEAD_DIM, pair, decay[:, cols]).astype(BF16)
    vt_ref[0, 0] = (_dot_nt(wvt_ref[...], h) + vones_ref[...]).astype(BF16)
    sgt_ref[0, 0] = _sigmoid(_dot_nt(wgt_ref[...], h)).astype(BF16)


def _fox_placement():
    place = np.zeros((LANES, 2 * FOX_HEADS * FOX_SLOT), np.float32)
    k_off = FOX_HEADS * FOX_SLOT
    for h in range(FOX_HEADS):
        base = h * FOX_SLOT + FOX_HEAD_DIM
        for j in range(N_SPLIT):
            place[j * FOX_HEADS + h, base + j] = 1.0
            place[ONES_LANE, base + N_SPLIT + j] = 1.0
            place[ONES_LANE, k_off + base + j] = 1.0
            place[j * FOX_HEADS + h, k_off + base + N_SPLIT + j] = -1.0
    return jnp.asarray(place, BF16)


def _fox_proj(x, shift, scale, norm_g, w_in, b_f):
    b, s, d = x.shape
    hds, dh = FOX_HEADS, FOX_HEAD_DIM
    tm = ATT_BLOCK
    att_scale = dh ** -0.5

    wqk = jnp.concatenate([w_in[:, 0:d] * (att_scale * LOG2E), w_in[:, d:2 * d]], axis=1).astype(BF16)
    wv = w_in[:, 2 * d:3 * d].T.reshape(hds, dh, d)
    pad = FOX_VROWS - dh
    wvt = jnp.concatenate([wv, jnp.zeros((hds, pad, d), F32)], axis=1).reshape(hds * FOX_VROWS, d).astype(BF16)
    vones = jnp.concatenate([jnp.zeros((hds, dh, 1), F32), jnp.ones((hds, pad, 1), F32)],
                            axis=1).reshape(hds * FOX_VROWS, 1)
    wgt = w_in[:, 3 * d:4 * d].T.astype(BF16)
    wf = jnp.zeros((d, LANES), F32).at[:, :hds].set(w_in[:, 4 * d:]).astype(BF16)
    bf = jnp.zeros((1, LANES), F32).at[0, :hds].set(b_f)
    tri = jnp.tril(jnp.ones((tm, tm), F32)).astype(BF16)
    qw = hds * FOX_SLOT
    vrows = hds * FOX_VROWS
    return pl.pallas_call(
        _fox_proj_kernel,
        grid=(b, s // tm),
        in_specs=[
            _row_spec(tm, d), _batch_vec_spec(d), _batch_vec_spec(d), _resident((1, d)),
            _resident((d, 2 * d)), _resident((vrows, d)), _resident((vrows, 1)), _resident((d, d)),
            _resident((d, LANES)), _resident((1, LANES)), _resident((tm, tm)), _resident((LANES, 2 * qw)),
        ],
        out_specs=[_row_spec(tm, qw), _row_spec(tm, qw), _block_major_spec(vrows, tm),
                   _block_major_spec(d, tm)],
        out_shape=[
            jax.ShapeDtypeStruct((b, s, qw), BF16), jax.ShapeDtypeStruct((b, s, qw), BF16),
            jax.ShapeDtypeStruct((b, s // tm, vrows, tm), BF16), jax.ShapeDtypeStruct((b, s // tm, d, tm), BF16),
        ],
        scratch_shapes=[pltpu.VMEM((1, LANES), F32)],
        compiler_params=_params("parallel", "arbitrary"),
        name="fox_proj",
    )(x, shift, scale, norm_g.reshape(1, d), wqk, wvt, vones, wgt, wf, bf, tri, _fox_placement())


def _fox_attn_kernel(qa_ref, ka_ref, vt_ref, sgt_ref, o_ref,
                     acc_ref, m_ref, bias_ref, shift_ref, alpha_ref, s_ref, *, blk, nblk):
    dh = FOX_HEAD_DIM
    heads = range(2)

    kp = lax.broadcasted_iota(jnp.int32, (blk, blk), 0)
    qp = lax.broadcasted_iota(jnp.int32, (blk, blk), 1)
    bias_ref[...] = jnp.where(kp <= qp, 0.0, -jnp.inf)
    acc_ref[...] = jnp.zeros_like(acc_ref)
    m_ref[...] = jnp.full_like(m_ref, -jnp.inf)

    def scores(ij, buf, diagonal):
        i, j = ij
        q = qa_ref[0, pl.ds(pl.multiple_of(i * blk, blk), blk), :]
        k = ka_ref[0, pl.ds(pl.multiple_of(j * blk, blk), blk), :]
        for hh in heads:
            slot = slice(hh * FOX_SLOT, (hh + 1) * FOX_SLOT)
            s = _dot_nt(k[:, slot], q[:, slot])
            if diagonal:
                s = s + bias_ref[...]
            m_old = m_ref[i, hh]
            m_new = jnp.maximum(m_old, jnp.max(s, axis=0, keepdims=True))
            m_ref[i, hh] = m_new
            shift_ref[buf, hh] = m_new
            alpha_ref[buf, hh] = jnp.exp2(m_old - m_new)
            s_ref[buf, hh] = s

    def values(ij, buf, diagonal):
        i, j = ij
        vt = vt_ref[0, j]
        for hh in heads:
            p = jnp.exp2(s_ref[buf, hh] - shift_ref[buf, hh])
            pv = _dot(vt[hh * FOX_VROWS:(hh + 1) * FOX_VROWS, :], p.astype(BF16))
            acc = alpha_ref[buf, hh] * acc_ref[i, hh] + pv
            if diagonal:
                rows = slice(hh * dh, (hh + 1) * dh)
                gated = acc[:dh] * (1.0 / acc[dh:dh + 1]) * sgt_ref[0, i, rows, :].astype(F32)
                o_ref[0, i, rows, :] = gated.astype(BF16)
            else:
                acc_ref[i, hh] = acc

    def sweep(first, advance, n_pairs, diagonal, unroll):
        assert unroll % 2 == 0 and n_pairs > unroll

        def steps(carry, count):
            sc, va = carry
            for r in range(count):
                scores(sc, (r + 1) % 2, diagonal)
                values(va, r % 2, diagonal)
                sc, va = advance(sc), advance(va)
            return sc, va

        scores(first, 0, diagonal)
        carry = lax.fori_loop(0, (n_pairs - 1) // unroll, lambda t, c: steps(c, unroll),
                              (advance(first), first))
        tail = (n_pairs - 1) % unroll
        _, va = steps(carry, tail)
        values(va, tail % 2, diagonal)

    def next_below_diagonal(ij):
        i, j = ij
        wrap = j + 1 == i
        return jnp.where(wrap, i + 1, i), jnp.where(wrap, 0, j + 1)

    zero, one = jnp.int32(0), jnp.int32(1)
    sweep((one, zero), next_below_diagonal, nblk * (nblk - 1) // 2, False, ATT_UNROLL)
    sweep((zero, zero), lambda ij: (ij[0] + 1, ij[1] + 1), nblk, True, ATT_UNROLL)


def _fox_attention(qa, ka, vt, sgt):
    b, nblk, d, blk = sgt.shape
    s = nblk * blk
    pairs = FOX_HEADS // 2
    pair_rows = 2 * FOX_HEAD_DIM
    kern = functools.partial(_fox_attn_kernel, blk=blk, nblk=nblk)
    return pl.pallas_call(
        kern,
        grid=(b, pairs),
        in_specs=[
            pl.BlockSpec((1, s, 2 * FOX_SLOT), lambda bb, p: (bb, 0, p)),
            pl.BlockSpec((1, s, 2 * FOX_SLOT), lambda bb, p: (bb, 0, p)),
            pl.BlockSpec((1, nblk, 2 * FOX_VROWS, blk), lambda bb, p: (bb, 0, p, 0)),
            pl.BlockSpec((1, nblk, pair_rows, blk), lambda bb, p: (bb, 0, p, 0)),
        ],
        out_specs=pl.BlockSpec((1, nblk, pair_rows, blk), lambda bb, p: (bb, 0, p, 0)),
        out_shape=jax.ShapeDtypeStruct((b, nblk, d, blk), BF16),
        scratch_shapes=[
            pltpu.VMEM((nblk, 2, FOX_VROWS, blk), F32),
            pltpu.VMEM((nblk, 2, 1, blk), F32),
            pltpu.VMEM((blk, blk), F32),
            pltpu.VMEM((2, 2, 1, blk), F32),
            pltpu.VMEM((2, 2, 1, blk), F32),
            pltpu.VMEM((2, 2, blk, blk), F32),
        ],
        compiler_params=_params("parallel", "parallel"),
        name="fox_attention",
    )(qa, ka, vt, sgt)


def _chunk_prefix_rows(x, chunk):
    n, lanes = x.shape
    row = lax.broadcasted_iota(jnp.int32, x.shape, 0) & (chunk - 1)
    shift = 1
    while shift < min(chunk, SUBLANES):
        x = x + jnp.where(row >= shift, pltpu.roll(x, shift, axis=0), 0.0)
        shift *= 2
    x = x.reshape(n // chunk, chunk, lanes)
    while shift < chunk:
        moved = jnp.concatenate([jnp.zeros((n // chunk, shift, lanes), F32), x[:, :chunk - shift]], axis=1)
        x = x + moved
        shift *= 2
    return x.reshape(n, lanes)


def _hgrn_proj_kernel(x_ref, shift_ref, scale_ref, ng_ref, w_ref, lbl_ref,
                      q_ref, kk_ref, g_ref, v_ref, sg_ref, *, d, layer, chunk):
    lg = lbl_ref[...]
    e = jnp.exp(lg - jnp.max(lg, axis=0, keepdims=True))
    sm = e / jnp.sum(e, axis=0, keepdims=True)
    lb = jnp.sum(sm[0:layer + 1], axis=0, keepdims=True) - sm[0:1]

    h = _modulated_norm(x_ref[0], ng_ref[...], scale_ref[0], shift_ref[0]).astype(BF16)
    q_ref[0] = _dot(h, w_ref[:, 0:d]).astype(BF16)
    f = lb + (1.0 - lb) * _sigmoid(_dot(h, w_ref[:, d:2 * d]))
    kk_ref[0] = (1.0 - f).astype(BF16)
    g_ref[0] = _chunk_prefix_rows(jnp.log(f), chunk) * LOG2E
    v_ref[0] = _silu(_dot(h, w_ref[:, 2 * d:3 * d])).astype(BF16)
    sg_ref[0] = _silu(_dot(h, w_ref[:, 3 * d:4 * d])).astype(BF16)


def _hgrn_proj(x, shift, scale, norm_g, w_in, lb_logits, layer):
    b, s, d = x.shape
    depth = lb_logits.shape[0]
    tm = ROW_TILE // 2
    assert tm % HGRN_CHUNK == 0
    kern = functools.partial(_hgrn_proj_kernel, d=d, layer=layer, chunk=HGRN_CHUNK)
    bf = jax.ShapeDtypeStruct((b, s, d), BF16)
    return pl.pallas_call(
        kern,
        grid=(b, s // tm),
        in_specs=[
            _row_spec(tm, d), _batch_vec_spec(d), _batch_vec_spec(d), _resident((1, d)),
            _resident((d, 4 * d)), _resident((depth, d)),
        ],
        out_specs=[_row_spec(tm, d)] * 5,
        out_shape=[bf, bf, jax.ShapeDtypeStruct((b, s, d), F32), bf, bf],
        compiler_params=_params("parallel", "parallel"),
        name="hgrn_proj",
    )(x, shift, scale, norm_g.reshape(1, d), w_in.astype(BF16), lb_logits.astype(F32))


def _pivot_rows(g, half):
    n, lanes = g.shape
    if 2 * half >= SUBLANES:
        g3 = g.reshape(n // (2 * half), 2 * half, lanes)
        piv = jnp.broadcast_to(g3[:, half - 1:half, :], g3.shape)
        return piv.reshape(n, lanes)
    g3 = g.reshape(n // SUBLANES, SUBLANES, lanes)
    sub = lax.broadcasted_iota(jnp.int32, g3.shape, 1)
    out = None
    for grp in range(SUBLANES // (2 * half)):
        r = grp * 2 * half + half - 1
        cand = jnp.broadcast_to(g3[:, r:r + 1, :], g3.shape)
        out = cand if out is None else jnp.where(sub >= grp * 2 * half, cand, out)
    return out.reshape(n, lanes)


def _interleave_rows(lower, upper, half):
    n, lanes = lower.shape
    if half >= SUBLANES:
        lo3 = lower.reshape(n // (2 * half), 2 * half, lanes)
        up3 = upper.reshape(n // (2 * half), 2 * half, lanes)
        return jnp.concatenate([lo3[:, :half], up3[:, half:]], axis=1).reshape(n, lanes)
    row = lax.broadcasted_iota(jnp.int32, lower.shape, 0)
    return jnp.where((row & half) != 0, upper, lower)


def _hgrn_chunk_kernel(q_ref, kk_ref, g_ref, v_ref, sg_ref, gn_ref, o_ref, state_ref, level_ref,
                       *, chunk, heads):
    n_levels = chunk.bit_length() - 1

    @pl.when(pl.program_id(2) == 0)
    def _():
        state_ref[...] = jnp.zeros_like(state_ref)
        t_idx = lax.broadcasted_iota(jnp.int32, (chunk, chunk), 0)
        s_idx = lax.broadcasted_iota(jnp.int32, (chunk, chunk), 1)
        diff = t_idx ^ s_idx
        level = jnp.full((chunk, chunk), -1, jnp.int32)
        for lv in range(n_levels):
            level = level + (diff >= (1 << lv)).astype(jnp.int32)
        level_ref[...] = jnp.where(t_idx > s_idx, level, jnp.where(t_idx == s_idx, n_levels, -1))

    kdim = HGRN_EXPAND
    level = level_ref[...]
    for hh in range(heads):
        cols = slice(hh * kdim, (hh + 1) * kdim)
        qb = q_ref[0, :, cols]
        kb = kk_ref[0, :, cols]
        q = qb.astype(F32)
        kk = kb.astype(F32)
        v = v_ref[0, :, cols]
        g = g_ref[0, :, cols]

        a = jnp.where(level == n_levels, _dot_nt(qb, kb), 0.0)
        for lv in range(n_levels):
            half = 1 << lv
            e = jnp.exp2(-jnp.abs(g - _pivot_rows(g, half)))
            y = (_interleave_rows(kk, q, half) * e).astype(BF16)
            a = jnp.where(level == lv, _dot_nt(y, y), a)

        g_last = g[chunk - 1:chunk, :]
        state_t = state_ref[hh]
        o = _dot_nt((q * jnp.exp2(g)).astype(BF16), state_t.astype(BF16)) + _dot(a.astype(BF16), v)
        k_tail = (kk * jnp.exp2(g_last - g)).astype(BF16)
        state_ref[hh] = state_t * jnp.exp2(g_last) + _dot_tn(v, k_tail)

        o = o * lax.rsqrt(jnp.mean(o * o, axis=-1, keepdims=True) + EPS)
        o = (o * gn_ref[:, cols]) * sg_ref[0, :, cols].astype(F32)
        o_ref[0, :, cols] = o.astype(BF16)


def _hgrn_chunks(q, kk, g, v, sg, g_norm):
    b, s, d = q.shape
    chunk, heads = HGRN_CHUNK, HGRN_HEADS_PER_STEP
    width = heads * HGRN_EXPAND
    spec = pl.BlockSpec((1, chunk, width), lambda bb, hp, c: (bb, c, hp))
    kern = functools.partial(_hgrn_chunk_kernel, chunk=chunk, heads=heads)
    return pl.pallas_call(
        kern,
        grid=(b, d // width, s // chunk),
        in_specs=[spec] * 5 + [pl.BlockSpec((1, width), lambda bb, hp, c: (0, hp))],
        out_specs=spec,
        out_shape=jax.ShapeDtypeStruct((b, s, d), BF16),
        scratch_shapes=[pltpu.VMEM((heads, HGRN_EXPAND, HGRN_EXPAND), F32),
                        pltpu.VMEM((chunk, chunk), jnp.int32)],
        compiler_params=_params("parallel", "parallel", "arbitrary"),
        name="hgrn_chunks",
    )(q, kk, g, v, sg, g_norm.reshape(1, d).astype(F32))


def kernel(x, c, ada_w, ada_b, norm_g, ffn_w_up, ffn_w_down, fox_w_in, fox_b_f, fox_w_out,
           hgrn_w_in, hgrn_norm_g, hgrn_w_out, hgrn_lb_logits, final_norm_g):
    b, s, d = x.shape
    depth = ada_w.shape[0]
    mod = _ada_modulation(c, ada_w, ada_b).reshape(depth, b, N_SUB, 3, 1, d)
    w_up, w_down = ffn_w_up.astype(BF16), ffn_w_down.astype(BF16)
    for i in range(depth):
        shift = lambda sub: mod[i, :, sub, 0]
        scale = lambda sub: mod[i, :, sub, 1]
        gate = lambda sub: mod[i, :, sub, 2]
        x = _ffn(x, shift(0), scale(0), gate(0), norm_g[i, 0], w_up, w_down, (i, 0),
                 final_norm_g, False)
        j = i // 2
        if i % 2 == 0:
            qa, ka, vt, sgt = _fox_proj(x, shift(1), scale(1), norm_g[i, 1], fox_w_in[j], fox_b_f[j])
            mixer = (_fox_attention(qa, ka, vt, sgt), gate(1), fox_w_out[j])
        else:
            q, kk, g, v, sg = _hgrn_proj(x, shift(1), scale(1), norm_g[i, 1], hgrn_w_in[j],
                                         hgrn_lb_logits, i)
            mixer = (_hgrn_chunks(q, kk, g, v, sg, hgrn_norm_g[j]), gate(1), hgrn_w_out[j])
        x = _ffn(x, shift(2), scale(2), gate(2), norm_g[i, 2], w_up, w_down, (i, 1),
                 final_norm_g, i == depth - 1, mixer)
    return x
```

```python
import functools

import jax
import jax.numpy as jnp
import numpy as np
from jax import lax
from jax.experimental import pallas as pl
from jax.experimental.pallas import tpu as pltpu

N_SUB = 3
FOX_HEADS = 16
FOX_HEAD_DIM = 64
HGRN_EXPAND = 128
EPS = 1e-6

LANES = 128
SUBLANES = 8
VMEM_LIMIT_BYTES = 56 * 1024 * 1024

ROW_TILE = 1024
FFN_CHUNK = 256
ATT_BLOCK = 512
ATT_UNROLL = 8
HGRN_CHUNK = 128
HGRN_HEADS_PER_STEP = 8
HGRN_PROJ_SUBTILE = 256
N_SPLIT = 3

BF16 = jnp.bfloat16
F32 = jnp.float32


def _params(*sem):
    return pltpu.CompilerParams(dimension_semantics=sem, vmem_limit_bytes=VMEM_LIMIT_BYTES)


def _resident(shape):
    zeros = (0,) * len(shape)
    return pl.BlockSpec(shape, lambda *_: zeros, pipeline_mode=pl.Buffered(1))


def _row_spec(tm, width):
    return pl.BlockSpec((1, tm, width), lambda b, i: (b, i, 0))


def _block_major_spec(features, tm):
    return pl.BlockSpec((1, 1, features, tm), lambda b, i: (b, i, 0, 0))


def _batch_vec_spec(width):
    return pl.BlockSpec((1, 1, width), lambda b, i: (b, 0, 0))


def _dot(a, b):
    return jnp.dot(a, b, preferred_element_type=F32)


def _dot_nt(a, b):
    return lax.dot_general(a, b, (((1,), (1,)), ((), ())), preferred_element_type=F32)


def _dot_tn(a, b):
    return lax.dot_general(a, b, (((0,), (0,)), ((), ())), preferred_element_type=F32)


def _sigmoid(x):
    return 1.0 / (1.0 + jnp.exp(-x))


def _silu(x):
    return x * _sigmoid(x)


def _modulated_norm(x, norm_g, scale, shift):
    y = x * lax.rsqrt(jnp.mean(x * x, axis=-1, keepdims=True) + EPS)
    return (y * norm_g) * (1.0 + scale) + shift


def _split_bf16(x, n):
    pieces = []
    rest = x
    for _ in range(n):
        p = rest.astype(BF16)
        pieces.append(p)
        rest = rest - p.astype(F32)
    return pieces


def _ada_kernel(c_ref, w_ref, b_ref, o_ref):
    cond = _silu(c_ref[...])
    y = jnp.dot(cond, w_ref[0], preferred_element_type=F32,
                precision=lax.Precision.HIGHEST)
    o_ref[0] = y + b_ref[0]


def _ada_modulation(c, ada_w, ada_b):
    depth, d, n = ada_w.shape
    b = c.shape[0]
    bp = -(-b // SUBLANES) * SUBLANES
    tn = n // 9 if n % (9 * LANES) == 0 else n
    c_pad = jnp.zeros((bp, d), F32).at[:b].set(c)
    out = pl.pallas_call(
        _ada_kernel,
        grid=(depth, n // tn),
        in_specs=[
            pl.BlockSpec((bp, d), lambda i, j: (0, 0)),
            pl.BlockSpec((1, d, tn), lambda i, j: (i, 0, j)),
            pl.BlockSpec((1, 1, tn), lambda i, j: (i, 0, j)),
        ],
        out_specs=pl.BlockSpec((1, bp, tn), lambda i, j: (i, 0, j)),
        out_shape=jax.ShapeDtypeStruct((depth, bp, n), F32),
        compiler_params=_params("parallel", "parallel"),
        name="ada_modulation",
    )(c_pad, ada_w, ada_b.reshape(depth, 1, n))
    return out[:, :b]


def _ffn_kernel(*refs, tf, mixer, final_norm):
    if mixer is None:
        x_ref, shift_ref, scale_ref, gate_ref, ng_ref, wup_ref, wdn_ref, fg_ref = refs[:8]
    else:
        (x_ref, mix_ref, mgate_ref, wout_ref,
         shift_ref, scale_ref, gate_ref, ng_ref, wup_ref, wdn_ref, fg_ref) = refs[:11]
    o_ref, h_ref, acc_ref = refs[-3:]
    wup_ref, wdn_ref = wup_ref.at[0, 0], wdn_ref.at[0, 0]
    hidden = wdn_ref.shape[0]

    x = x_ref[0]
    if mixer == "rows":
        x = x + mgate_ref[0] * _dot(mix_ref[0], wout_ref[...])
    elif mixer == "features":
        parts = [_dot_tn(mix_ref[0, blk], wout_ref[...]) for blk in range(mix_ref.shape[1])]
        x = x + mgate_ref[0] * jnp.concatenate(parts, axis=0)
    o_ref[0] = x
    h_ref[...] = _modulated_norm(x, ng_ref[...], scale_ref[0], shift_ref[0]).astype(BF16)

    for f in range(0, hidden, tf):
        a = _dot(h_ref[...], wup_ref[:, f:f + tf])
        b = _dot(h_ref[...], wup_ref[:, hidden + f:hidden + f + tf])
        down = _dot((_silu(a) * b).astype(BF16), wdn_ref[f:f + tf, :])
        if f == 0:
            acc_ref[...] = down
        else:
            acc_ref[...] += down

    y = o_ref[0] + (0.5 * gate_ref[0]) * acc_ref[...]
    if final_norm:
        y = (y * lax.rsqrt(jnp.mean(y * y, axis=-1, keepdims=True) + EPS)) * fg_ref[...]
    o_ref[0] = y


def _ffn(x, shift, scale, gate, norm_g, w_up, w_down, which, final_g, final_norm, mixer=None):
    b, s, d = x.shape
    f = w_down.shape[2]
    stack_index = tuple(which) + (0, 0)
    stack_spec = lambda rows, cols: pl.BlockSpec((1, 1, rows, cols), lambda bb, i: stack_index,
                                                 pipeline_mode=pl.Buffered(1))
    tf = FFN_CHUNK
    assert f % tf == 0
    tm = ROW_TILE
    operands, specs, mode = [x], [_row_spec(tm, d)], None
    if mixer is not None:
        o, mgate, w_out = mixer
        if o.ndim == 4:
            mode, blk = "features", o.shape[3]
            o_spec = pl.BlockSpec((1, tm // blk, d, blk), lambda bb, i: (bb, i, 0, 0))
        else:
            mode, o_spec = "rows", _row_spec(tm, d)
        operands += [o, mgate, w_out.astype(BF16)]
        specs += [o_spec, _batch_vec_spec(d), _resident((d, d))]
    operands += [shift, scale, gate, norm_g.reshape(1, d), w_up, w_down, final_g.reshape(1, d)]
    specs += [_batch_vec_spec(d), _batch_vec_spec(d), _batch_vec_spec(d), _resident((1, d)),
              stack_spec(d, 2 * f), stack_spec(f, d), _resident((1, d))]
    kern = functools.partial(_ffn_kernel, tf=tf, mixer=mode, final_norm=final_norm)
    return pl.pallas_call(
        kern,
        grid=(b, s // tm),
        in_specs=specs,
        out_specs=_row_spec(tm, d),
        out_shape=jax.ShapeDtypeStruct((b, s, d), F32),
        scratch_shapes=[pltpu.VMEM((tm, d), BF16), pltpu.VMEM((tm, d), F32)],
        compiler_params=_params("parallel", "parallel"),
        name="ffn" + ("" if mode is None else "_" + mode) + ("_final" if final_norm else ""),
    )(*operands)


FOX_SLOT = 2 * FOX_HEAD_DIM
ONES_LANE = N_SPLIT * FOX_HEADS
FOX_VROWS = FOX_HEAD_DIM + 16
LOG2E = 1.4426950408889634


def _fox_proj_kernel(x_ref, shift_ref, scale_ref, ng_ref, wqk_ref, wvt_ref, vones_ref, wgt_ref,
                     wf_ref, bf_ref, tri_ref, place_ref, qa_ref, ka_ref, vt_ref, sgt_ref, carry_ref):
    @pl.when(pl.program_id(1) == 0)
    def _():
        carry_ref[...] = jnp.zeros_like(carry_ref)

    h = _modulated_norm(x_ref[0], ng_ref[...], scale_ref[0], shift_ref[0]).astype(BF16)
    z = _dot(h, wf_ref[...]) + bf_ref[...]
    logf = jnp.minimum(z, 0.0) - jnp.log(1.0 + jnp.exp(-jnp.abs(z)))
    pieces = jnp.concatenate(_split_bf16(logf, N_SPLIT), axis=1)
    part = _dot(tri_ref[...], pieces)
    cum = carry_ref[...] + ((part[:, :LANES] + part[:, LANES:2 * LANES]) + part[:, 2 * LANES:])
    carry_ref[...] = cum[-1:, :]

    lane = lax.broadcasted_iota(jnp.int32, cum.shape, 1)
    operand = jnp.where(lane == ONES_LANE, 1.0, 0.0).astype(F32)
    for j, p in enumerate(_split_bf16(cum * LOG2E, N_SPLIT)):
        pf = p.astype(F32)
        shifted = pf if j == 0 else pltpu.roll(pf, j * FOX_HEADS, axis=1)
        operand = jnp.where((lane >= j * FOX_HEADS) & (lane < (j + 1) * FOX_HEADS), shifted, operand)
    operand = operand.astype(BF16)
    d = wqk_ref.shape[0]
    slot_lane = lax.broadcasted_iota(jnp.int32, (operand.shape[0], FOX_SLOT), 1)
    for out_ref, w_cols, place_cols in ((qa_ref, 0, 0), (ka_ref, d, FOX_HEADS * FOX_SLOT)):
        packed = _dot(h, wqk_ref[:, w_cols:w_cols + d])
        decay = _dot(operand, place_ref[:, place_cols:place_cols + FOX_HEADS * FOX_SLOT])
        for hd in range(FOX_HEADS):
            pair = packed[:, (hd // 2) * FOX_SLOT:(hd // 2 + 1) * FOX_SLOT]
            if hd % 2:
                pair = pltpu.roll(pair, FOX_HEAD_DIM, axis=1)
            cols = slice(hd * FOX_SLOT, (hd + 1) * FOX_SLOT)
            out_ref[0, :, cols] = jnp.where(slot_lane < FOX_HEAD_DIM, pair, decay[:, cols]).astype(BF16)
    vt_ref[0, 0] = (_dot_nt(wvt_ref[...], h) + vones_ref[...]).astype(BF16)
    sgt_ref[0, 0] = _sigmoid(_dot_nt(wgt_ref[...], h)).astype(BF16)


def _fox_placement():
    place = np.zeros((LANES, 2 * FOX_HEADS * FOX_SLOT), np.float32)
    k_off = FOX_HEADS * FOX_SLOT
    for h in range(FOX_HEADS):
        base = h * FOX_SLOT + FOX_HEAD_DIM
        for j in range(N_SPLIT):
            place[j * FOX_HEADS + h, base + j] = 1.0
            place[ONES_LANE, base + N_SPLIT + j] = 1.0
            place[ONES_LANE, k_off + base + j] = 1.0
            place[j * FOX_HEADS + h, k_off + base + N_SPLIT + j] = -1.0
    return jnp.asarray(place, BF16)


def _fox_proj(x, shift, scale, norm_g, w_in, b_f):
    b, s, d = x.shape
    hds, dh = FOX_HEADS, FOX_HEAD_DIM
    tm = ATT_BLOCK
    att_scale = dh ** -0.5

    wqk = jnp.concatenate([w_in[:, 0:d] * (att_scale * LOG2E), w_in[:, d:2 * d]], axis=1).astype(BF16)
    wv = w_in[:, 2 * d:3 * d].T.reshape(hds, dh, d)
    pad = FOX_VROWS - dh
    wvt = jnp.concatenate([wv, jnp.zeros((hds, pad, d), F32)], axis=1).reshape(hds * FOX_VROWS, d).astype(BF16)
    vones = jnp.concatenate([jnp.zeros((hds, dh, 1), F32), jnp.ones((hds, pad, 1), F32)],
                            axis=1).reshape(hds * FOX_VROWS, 1)
    wgt = w_in[:, 3 * d:4 * d].T.astype(BF16)
    wf = jnp.zeros((d, LANES), F32).at[:, :hds].set(w_in[:, 4 * d:]).astype(BF16)
    bf = jnp.zeros((1, LANES), F32).at[0, :hds].set(b_f)
    tri = jnp.tril(jnp.ones((tm, tm), F32)).astype(BF16)
    qw = hds * FOX_SLOT
    vrows = hds * FOX_VROWS
    return pl.pallas_call(
        _fox_proj_kernel,
        grid=(b, s // tm),
        in_specs=[
            _row_spec(tm, d), _batch_vec_spec(d), _batch_vec_spec(d), _resident((1, d)),
            _resident((d, 2 * d)), _resident((vrows, d)), _resident((vrows, 1)), _resident((d, d)),
            _resident((d, LANES)), _resident((1, LANES)), _resident((tm, tm)), _resident((LANES, 2 * qw)),
        ],
        out_specs=[_row_spec(tm, qw), _row_spec(tm, qw), _block_major_spec(vrows, tm),
                   _block_major_spec(d, tm)],
        out_shape=[
            jax.ShapeDtypeStruct((b, s, qw), BF16), jax.ShapeDtypeStruct((b, s, qw), BF16),
            jax.ShapeDtypeStruct((b, s // tm, vrows, tm), BF16), jax.ShapeDtypeStruct((b, s // tm, d, tm), BF16),
        ],
        scratch_shapes=[pltpu.VMEM((1, LANES), F32)],
        compiler_params=_params("parallel", "arbitrary"),
        name="fox_proj",
    )(x, shift, scale, norm_g.reshape(1, d), wqk, wvt, vones, wgt, wf, bf, tri, _fox_placement())


def _fox_attn_kernel(qa_ref, ka_ref, vt_ref, sgt_ref, o_ref,
                     acc_ref, m_ref, bias_ref, shift_ref, alpha_ref, s_ref, *, blk, nblk):
    dh = FOX_HEAD_DIM
    heads = range(2)

    kp = lax.broadcasted_iota(jnp.int32, (blk, blk), 0)
    qp = lax.broadcasted_iota(jnp.int32, (blk, blk), 1)
    bias_ref[...] = jnp.where(kp <= qp, 0.0, -jnp.inf)
    acc_ref[...] = jnp.zeros_like(acc_ref)
    m_ref[...] = jnp.full_like(m_ref, -jnp.inf)

    def scores(ij, buf, diagonal):
        i, j = ij
        q = qa_ref[0, pl.ds(pl.multiple_of(i * blk, blk), blk), :]
        k = ka_ref[0, pl.ds(pl.multiple_of(j * blk, blk), blk), :]
        for hh in heads:
            slot = slice(hh * FOX_SLOT, (hh + 1) * FOX_SLOT)
            s = _dot_nt(k[:, slot], q[:, slot])
            if diagonal:
                s = s + bias_ref[...]
            m_old = m_ref[i, hh]
            m_new = jnp.maximum(m_old, jnp.max(s, axis=0, keepdims=True))
            m_ref[i, hh] = m_new
            shift_ref[buf, hh] = m_new
            alpha_ref[buf, hh] = jnp.exp2(m_old - m_new)
            s_ref[buf, hh] = s

    def values(ij, buf, diagonal):
        i, j = ij
        vt = vt_ref[0, j]
        for hh in heads:
            p = jnp.exp2(s_ref[buf, hh] - shift_ref[buf, hh])
            pv = _dot(vt[hh * FOX_VROWS:(hh + 1) * FOX_VROWS, :], p.astype(BF16))
            acc = alpha_ref[buf, hh] * acc_ref[i, hh] + pv
            if diagonal:
                rows = slice(hh * dh, (hh + 1) * dh)
                gated = acc[:dh] * (1.0 / acc[dh:dh + 1]) * sgt_ref[0, i, rows, :].astype(F32)
                o_ref[0, i, rows, :] = gated.astype(BF16)
            else:
                acc_ref[i, hh] = acc

    def sweep(first, advance, n_pairs, diagonal, unroll):
        assert unroll % 2 == 0 and n_pairs > unroll

        def steps(carry, count):
            sc, va = carry
            for r in range(count):
                scores(sc, (r + 1) % 2, diagonal)
                values(va, r % 2, diagonal)
                sc, va = advance(sc), advance(va)
            return sc, va

        scores(first, 0, diagonal)
        carry = lax.fori_loop(0, (n_pairs - 1) // unroll, lambda t, c: steps(c, unroll),
                              (advance(first), first))
        tail = (n_pairs - 1) % unroll
        _, va = steps(carry, tail)
        values(va, tail % 2, diagonal)

    def next_below_diagonal(ij):
        i, j = ij
        wrap = j + 1 == i
        return jnp.where(wrap, i + 1, i), jnp.where(wrap, 0, j + 1)

    zero, one = jnp.int32(0), jnp.int32(1)
    sweep((one, zero), next_below_diagonal, nblk * (nblk - 1) // 2, False, ATT_UNROLL)
    sweep((zero, zero), lambda ij: (ij[0] + 1, ij[1] + 1), nblk, True, ATT_UNROLL)


def _fox_attention(qa, ka, vt, sgt):
    b, nblk, d, blk = sgt.shape
    s = nblk * blk
    pairs = FOX_HEADS // 2
    pair_rows = 2 * FOX_HEAD_DIM
    kern = functools.partial(_fox_attn_kernel, blk=blk, nblk=nblk)
    return pl.pallas_call(
        kern,
        grid=(b, pairs),
        in_specs=[
            pl.BlockSpec((1, s, 2 * FOX_SLOT), lambda bb, p: (bb, 0, p)),
            pl.BlockSpec((1, s, 2 * FOX_SLOT), lambda bb, p: (bb, 0, p)),
            pl.BlockSpec((1, nblk, 2 * FOX_VROWS, blk), lambda bb, p: (bb, 0, p, 0)),
            pl.BlockSpec((1, nblk, pair_rows, blk), lambda bb, p: (bb, 0, p, 0)),
        ],
        out_specs=pl.BlockSpec((1, nblk, pair_rows, blk), lambda bb, p: (bb, 0, p, 0)),
        out_shape=jax.ShapeDtypeStruct((b, nblk, d, blk), BF16),
        scratch_shapes=[
            pltpu.VMEM((nblk, 2, FOX_VROWS, blk), F32),
            pltpu.VMEM((nblk, 2, 1, blk), F32),
            pltpu.VMEM((blk, blk), F32),
            pltpu.VMEM((2, 2, 1, blk), F32),
            pltpu.VMEM((2, 2, 1, blk), F32),
            pltpu.VMEM((2, 2, blk, blk), F32),
        ],
        compiler_params=_params("parallel", "parallel"),
        name="fox_attention",
    )(qa, ka, vt, sgt)


def _chunk_prefix_rows(x, chunk):
    n, lanes = x.shape
    row = lax.broadcasted_iota(jnp.int32, x.shape, 0) & (chunk - 1)
    shift = 1
    while shift < min(chunk, SUBLANES):
        x = x + jnp.where(row >= shift, pltpu.roll(x, shift, axis=0), 0.0)
        shift *= 2
    x = x.reshape(n // chunk, chunk, lanes)
    while shift < chunk:
        moved = jnp.concatenate([jnp.zeros((n // chunk, shift, lanes), F32), x[:, :chunk - shift]], axis=1)
        x = x + moved
        shift *= 2
    return x.reshape(n, lanes)


def _hgrn_proj_kernel(x_ref, shift_ref, scale_ref, ng_ref, w_ref, lbl_ref,
                      q_ref, kk_ref, g_ref, v_ref, sg_ref, *, d, layer, chunk):
    lg = lbl_ref[...]
    e = jnp.exp(lg - jnp.max(lg, axis=0, keepdims=True))
    sm = e / jnp.sum(e, axis=0, keepdims=True)
    lb = jnp.sum(sm[0:layer + 1], axis=0, keepdims=True) - sm[0:1]

    for r0 in range(0, x_ref.shape[1], HGRN_PROJ_SUBTILE):
        rows = slice(r0, r0 + HGRN_PROJ_SUBTILE)
        h = _modulated_norm(x_ref[0, rows, :], ng_ref[...], scale_ref[0], shift_ref[0]).astype(BF16)
        q_ref[0, rows, :] = _dot(h, w_ref[:, 0:d]).astype(BF16)
        f = lb + (1.0 - lb) * _sigmoid(_dot(h, w_ref[:, d:2 * d]))
        kk_ref[0, rows, :] = (1.0 - f).astype(BF16)
        g_ref[0, rows, :] = _chunk_prefix_rows(jnp.log(f), chunk) * LOG2E
        v_ref[0, rows, :] = _silu(_dot(h, w_ref[:, 2 * d:3 * d])).astype(BF16)
        sg_ref[0, rows, :] = _silu(_dot(h, w_ref[:, 3 * d:4 * d])).astype(BF16)


def _hgrn_proj(x, shift, scale, norm_g, w_in, lb_logits, layer):
    b, s, d = x.shape
    depth = lb_logits.shape[0]
    tm = ROW_TILE // 2
    assert tm % HGRN_PROJ_SUBTILE == 0 and HGRN_PROJ_SUBTILE % HGRN_CHUNK == 0
    kern = functools.partial(_hgrn_proj_kernel, d=d, layer=layer, chunk=HGRN_CHUNK)
    bf = jax.ShapeDtypeStruct((b, s, d), BF16)
    return pl.pallas_call(
        kern,
        grid=(b, s // tm),
        in_specs=[
            _row_spec(tm, d), _batch_vec_spec(d), _batch_vec_spec(d), _resident((1, d)),
            _resident((d, 4 * d)), _resident((depth, d)),
        ],
        out_specs=[_row_spec(tm, d)] * 5,
        out_shape=[bf, bf, jax.ShapeDtypeStruct((b, s, d), F32), bf, bf],
        compiler_params=_params("parallel", "parallel"),
        name="hgrn_proj",
    )(x, shift, scale, norm_g.reshape(1, d), w_in.astype(BF16), lb_logits.astype(F32))


def _pivot_rows(g, half):
    n, lanes = g.shape
    if 2 * half >= SUBLANES:
        g3 = g.reshape(n // (2 * half), 2 * half, lanes)
        piv = jnp.broadcast_to(g3[:, half - 1:half, :], g3.shape)
        return piv.reshape(n, lanes)
    g3 = g.reshape(n // SUBLANES, SUBLANES, lanes)
    sub = lax.broadcasted_iota(jnp.int32, g3.shape, 1)
    out = None
    for grp in range(SUBLANES // (2 * half)):
        r = grp * 2 * half + half - 1
        cand = jnp.broadcast_to(g3[:, r:r + 1, :], g3.shape)
        out = cand if out is None else jnp.where(sub >= grp * 2 * half, cand, out)
    return out.reshape(n, lanes)


def _minus_abs(x):
    bits = pltpu.bitcast(x, jnp.uint32) | jnp.uint32(0x80000000)
    return pltpu.bitcast(bits, F32)


def _interleave_rows(lower, upper, half):
    n, lanes = lower.shape
    if half >= SUBLANES:
        lo3 = lower.reshape(n // (2 * half), 2 * half, lanes)
        up3 = upper.reshape(n // (2 * half), 2 * half, lanes)
        return jnp.concatenate([lo3[:, :half], up3[:, half:]], axis=1).reshape(n, lanes)
    row = lax.broadcasted_iota(jnp.int32, lower.shape, 0)
    return jnp.where((row & half) != 0, upper, lower)


def _hgrn_chunk_kernel(q_ref, kk_ref, g_ref, v_ref, sg_ref, gn_ref, o_ref, state_ref, level_ref,
                       *, chunk, heads):
    n_levels = chunk.bit_length() - 1

    @pl.when(pl.program_id(2) == 0)
    def _():
        state_ref[...] = jnp.zeros_like(state_ref)
        t_idx = lax.broadcasted_iota(jnp.int32, (chunk, chunk), 0)
        s_idx = lax.broadcasted_iota(jnp.int32, (chunk, chunk), 1)
        diff = t_idx ^ s_idx
        level = jnp.full((chunk, chunk), -1, jnp.int32)
        for lv in range(n_levels):
            level = level + (diff >= (1 << lv)).astype(jnp.int32)
        level_ref[...] = jnp.where(t_idx > s_idx, level, jnp.where(t_idx == s_idx, n_levels, -1))

    kdim = HGRN_EXPAND
    level = level_ref[...]
    for hh in range(heads):
        cols = slice(hh * kdim, (hh + 1) * kdim)
        qb = q_ref[0, :, cols]
        kb = kk_ref[0, :, cols]
        q = qb.astype(F32)
        kk = kb.astype(F32)
        v = v_ref[0, :, cols]
        g = g_ref[0, :, cols]

        a = jnp.where(level == n_levels, _dot_nt(qb, kb), 0.0)
        for lv in range(n_levels):
            half = 1 << lv
            e = jnp.exp2(_minus_abs(g - _pivot_rows(g, half)))
            y = (_interleave_rows(kk, q, half) * e).astype(BF16)
            a = jnp.where(level == lv, _dot_nt(y, y), a)

        g_last = g[chunk - 1:chunk, :]
        state_t = state_ref[hh]
        o = _dot_nt((q * jnp.exp2(g)).astype(BF16), state_t.astype(BF16)) + _dot(a.astype(BF16), v)
        k_tail = (kk * jnp.exp2(g_last - g)).astype(BF16)
        state_ref[hh] = state_t * jnp.exp2(g_last) + _dot_tn(v, k_tail)

        o = o * lax.rsqrt(jnp.mean(o * o, axis=-1, keepdims=True) + EPS)
        o = (o * gn_ref[:, cols]) * sg_ref[0, :, cols].astype(F32)
        o_ref[0, :, cols] = o.astype(BF16)


def _hgrn_chunks(q, kk, g, v, sg, g_norm):
    b, s, d = q.shape
    chunk, heads = HGRN_CHUNK, HGRN_HEADS_PER_STEP
    width = heads * HGRN_EXPAND
    spec = pl.BlockSpec((1, chunk, width), lambda bb, hp, c: (bb, c, hp))
    kern = functools.partial(_hgrn_chunk_kernel, chunk=chunk, heads=heads)
    return pl.pallas_call(
        kern,
        grid=(b, d // width, s // chunk),
        in_specs=[spec] * 5 + [pl.BlockSpec((1, width), lambda bb, hp, c: (0, hp))],
        out_specs=spec,
        out_shape=jax.ShapeDtypeStruct((b, s, d), BF16),
        scratch_shapes=[pltpu.VMEM((heads, HGRN_EXPAND, HGRN_EXPAND), F32),
                        pltpu.VMEM((chunk, chunk), jnp.int32)],
        compiler_params=_params("parallel", "parallel", "arbitrary"),
        name="hgrn_chunks",
    )(q, kk, g, v, sg, g_norm.reshape(1, d).astype(F32))


def kernel(x, c, ada_w, ada_b, norm_g, ffn_w_up, ffn_w_down, fox_w_in, fox_b_f, fox_w_out,
           hgrn_w_in, hgrn_norm_g, hgrn_w_out, hgrn_lb_logits, final_norm_g):
    b, s, d = x.shape
    depth = ada_w.shape[0]
    mod = _ada_modulation(c, ada_w, ada_b).reshape(depth, b, N_SUB, 3, 1, d)
    w_up, w_down = ffn_w_up.astype(BF16), ffn_w_down.astype(BF16)
    for i in range(depth):
        shift = lambda sub: mod[i, :, sub, 0]
        scale = lambda sub: mod[i, :, sub, 1]
        gate = lambda sub: mod[i, :, sub, 2]
        x = _ffn(x, shift(0), scale(0), gate(0), norm_g[i, 0], w_up, w_down, (i, 0),
                 final_norm_g, False)
        j = i // 2
        if i % 2 == 0:
            qa, ka, vt, sgt = _fox_proj(x, shift(1), scale(1), norm_g[i, 1], fox_w_in[j], fox_b_f[j])
            mixer = (_fox_attention(qa, ka, vt, sgt), gate(1), fox_w_out[j])
        else:
            q, kk, g, v, sg = _hgrn_proj(x, shift(1), scale(1), norm_g[i, 1], hgrn_w_in[j],
                                         hgrn_lb_logits, i)
            mixer = (_hgrn_chunks(q, kk, g, v, sg, hgrn_norm_g[j]), gate(1), hgrn_w_out[j])
        x = _ffn(x, shift(2), scale(2), gate(2), norm_g[i, 2], w_up, w_down, (i, 1),
                 final_norm_g, i == depth - 1, mixer)
    return x
```

```python
import functools

import jax
import jax.numpy as jnp
import numpy as np
from jax import lax
from jax.experimental import pallas as pl
from jax.experimental.pallas import tpu as pltpu

N_SUB = 3
FOX_HEADS = 16
FOX_HEAD_DIM = 64
HGRN_EXPAND = 128
EPS = 1e-6

LANES = 128
SUBLANES = 8
VMEM_LIMIT_BYTES = 56 * 1024 * 1024

ROW_TILE = 1024
FFN_CHUNK = 256
ATT_BLOCK = 512
ATT_UNROLL = 8
HGRN_CHUNK = 128
HGRN_HEADS_PER_STEP = 8
HGRN_PROJ_SUBTILE = 256
N_SPLIT = 3

BF16 = jnp.bfloat16
F32 = jnp.float32


def _params(*sem):
    return pltpu.CompilerParams(dimension_semantics=sem, vmem_limit_bytes=VMEM_LIMIT_BYTES)


def _resident(shape):
    zeros = (0,) * len(shape)
    return pl.BlockSpec(shape, lambda *_: zeros, pipeline_mode=pl.Buffered(1))


def _row_spec(tm, width):
    return pl.BlockSpec((1, tm, width), lambda b, i: (b, i, 0))


def _block_major_spec(features, tm):
    return pl.BlockSpec((1, 1, features, tm), lambda b, i: (b, i, 0, 0))


def _batch_vec_spec(width):
    return pl.BlockSpec((1, 1, width), lambda b, i: (b, 0, 0))


def _dot(a, b):
    return jnp.dot(a, b, preferred_element_type=F32)


def _dot_nt(a, b):
    return lax.dot_general(a, b, (((1,), (1,)), ((), ())), preferred_element_type=F32)


def _dot_tn(a, b):
    return lax.dot_general(a, b, (((0,), (0,)), ((), ())), preferred_element_type=F32)


def _sigmoid(x):
    return 1.0 / (1.0 + jnp.exp(-x))


def _silu(x):
    return x * _sigmoid(x)


def _modulated_norm(x, norm_g, scale, shift):
    y = x * lax.rsqrt(jnp.mean(x * x, axis=-1, keepdims=True) + EPS)
    return (y * norm_g) * (1.0 + scale) + shift


def _split_bf16(x, n):
    pieces = []
    rest = x
    for _ in range(n):
        p = rest.astype(BF16)
        pieces.append(p)
        rest = rest - p.astype(F32)
    return pieces


def _ada_kernel(c_ref, w_ref, b_ref, o_ref):
    cond = _silu(c_ref[...])
    y = jnp.dot(cond, w_ref[0], preferred_element_type=F32,
                precision=lax.Precision.HIGHEST)
    o_ref[0] = y + b_ref[0]


def _ada_modulation(c, ada_w, ada_b):
    depth, d, n = ada_w.shape
    b = c.shape[0]
    bp = -(-b // SUBLANES) * SUBLANES
    tn = n // 9 if n % (9 * LANES) == 0 else n
    c_pad = jnp.zeros((bp, d), F32).at[:b].set(c)
    out = pl.pallas_call(
        _ada_kernel,
        grid=(depth, n // tn),
        in_specs=[
            pl.BlockSpec((bp, d), lambda i, j: (0, 0)),
            pl.BlockSpec((1, d, tn), lambda i, j: (i, 0, j)),
            pl.BlockSpec((1, 1, tn), lambda i, j: (i, 0, j)),
        ],
        out_specs=pl.BlockSpec((1, bp, tn), lambda i, j: (i, 0, j)),
        out_shape=jax.ShapeDtypeStruct((depth, bp, n), F32),
        compiler_params=_params("parallel", "parallel"),
        name="ada_modulation",
    )(c_pad, ada_w, ada_b.reshape(depth, 1, n))
    return out[:, :b]


def _ffn_kernel(*refs, tf, mixer, final_norm):
    if mixer is None:
        x_ref, shift_ref, scale_ref, gate_ref, ng_ref, wup_ref, wdn_ref, fg_ref = refs[:8]
    else:
        (x_ref, mix_ref, mgate_ref, wout_ref,
         shift_ref, scale_ref, gate_ref, ng_ref, wup_ref, wdn_ref, fg_ref) = refs[:11]
    o_ref, h_ref, acc_ref = refs[-3:]
    wup_ref, wdn_ref = wup_ref.at[0, 0], wdn_ref.at[0, 0]
    hidden = wdn_ref.shape[0]

    x = x_ref[0]
    if mixer == "rows":
        x = x + mgate_ref[0] * _dot(mix_ref[0], wout_ref[...])
    elif mixer == "features":
        parts = [_dot_tn(mix_ref[0, blk], wout_ref[...]) for blk in range(mix_ref.shape[1])]
        x = x + mgate_ref[0] * jnp.concatenate(parts, axis=0)
    o_ref[0] = x
    h_ref[...] = _modulated_norm(x, ng_ref[...], scale_ref[0], shift_ref[0]).astype(BF16)

    for f in range(0, hidden, tf):
        a = _dot(h_ref[...], wup_ref[:, f:f + tf])
        b = _dot(h_ref[...], wup_ref[:, hidden + f:hidden + f + tf])
        down = _dot((_silu(a) * b).astype(BF16), wdn_ref[f:f + tf, :])
        if f == 0:
            acc_ref[...] = down
        else:
            acc_ref[...] += down

    y = o_ref[0] + (0.5 * gate_ref[0]) * acc_ref[...]
    if final_norm:
        y = (y * lax.rsqrt(jnp.mean(y * y, axis=-1, keepdims=True) + EPS)) * fg_ref[...]
    o_ref[0] = y


def _ffn(x, shift, scale, gate, norm_g, w_up, w_down, which, final_g, final_norm, mixer=None):
    b, s, d = x.shape
    f = w_down.shape[2]
    stack_index = tuple(which) + (0, 0)
    stack_spec = lambda rows, cols: pl.BlockSpec((1, 1, rows, cols), lambda bb, i: stack_index,
                                                 pipeline_mode=pl.Buffered(1))
    tf = FFN_CHUNK
    assert f % tf == 0
    tm = ROW_TILE
    operands, specs, mode = [x], [_row_spec(tm, d)], None
    if mixer is not None:
        o, mgate, w_out = mixer
        if o.ndim == 4:
            mode, blk = "features", o.shape[3]
            o_spec = pl.BlockSpec((1, tm // blk, d, blk), lambda bb, i: (bb, i, 0, 0))
        else:
            mode, o_spec = "rows", _row_spec(tm, d)
        operands += [o, mgate, w_out.astype(BF16)]
        specs += [o_spec, _batch_vec_spec(d), _resident((d, d))]
    operands += [shift, scale, gate, norm_g.reshape(1, d), w_up, w_down, final_g.reshape(1, d)]
    specs += [_batch_vec_spec(d), _batch_vec_spec(d), _batch_vec_spec(d), _resident((1, d)),
              stack_spec(d, 2 * f), stack_spec(f, d), _resident((1, d))]
    kern = functools.partial(_ffn_kernel, tf=tf, mixer=mode, final_norm=final_norm)
    return pl.pallas_call(
        kern,
        grid=(b, s // tm),
        in_specs=specs,
        out_specs=_row_spec(tm, d),
        out_shape=jax.ShapeDtypeStruct((b, s, d), F32),
        scratch_shapes=[pltpu.VMEM((tm, d), BF16), pltpu.VMEM((tm, d), F32)],
        compiler_params=_params("parallel", "parallel"),
        name="ffn" + ("" if mode is None else "_" + mode) + ("_final" if final_norm else ""),
    )(*operands)


FOX_SLOT = 2 * FOX_HEAD_DIM
ONES_LANE = N_SPLIT * FOX_HEADS
FOX_VROWS = FOX_HEAD_DIM + 16
LOG2E = 1.4426950408889634


def _fox_proj_kernel(x_ref, shift_ref, scale_ref, ng_ref, wqk_ref, wvt_ref, vones_ref, wgt_ref,
                     wf_ref, bf_ref, tri_ref, place_ref, qa_ref, ka_ref, vt_ref, sgt_ref, carry_ref):
    @pl.when(pl.program_id(1) == 0)
    def _():
        carry_ref[...] = jnp.zeros_like(carry_ref)

    h = _modulated_norm(x_ref[0], ng_ref[...], scale_ref[0], shift_ref[0]).astype(BF16)
    z = _dot(h, wf_ref[...]) + bf_ref[...]
    logf = jnp.minimum(z, 0.0) - jnp.log(1.0 + jnp.exp(-jnp.abs(z)))
    pieces = jnp.concatenate(_split_bf16(logf, N_SPLIT), axis=1)
    part = _dot(tri_ref[...], pieces)
    cum = carry_ref[...] + ((part[:, :LANES] + part[:, LANES:2 * LANES]) + part[:, 2 * LANES:])
    carry_ref[...] = cum[-1:, :]

    lane = lax.broadcasted_iota(jnp.int32, cum.shape, 1)
    operand = jnp.where(lane == ONES_LANE, 1.0, 0.0).astype(F32)
    for j, p in enumerate(_split_bf16(cum * LOG2E, N_SPLIT)):
        pf = p.astype(F32)
        shifted = pf if j == 0 else pltpu.roll(pf, j * FOX_HEADS, axis=1)
        operand = jnp.where((lane >= j * FOX_HEADS) & (lane < (j + 1) * FOX_HEADS), shifted, operand)
    operand = operand.astype(BF16)
    d = wqk_ref.shape[0]
    slot_lane = lax.broadcasted_iota(jnp.int32, (operand.shape[0], FOX_SLOT), 1)
    for out_ref, w_cols, place_cols in ((qa_ref, 0, 0), (ka_ref, d, FOX_HEADS * FOX_SLOT)):
        packed = _dot(h, wqk_ref[:, w_cols:w_cols + d])
        decay = _dot(operand, place_ref[:, place_cols:place_cols + FOX_HEADS * FOX_SLOT])
        for hd in range(FOX_HEADS):
            pair = packed[:, (hd // 2) * FOX_SLOT:(hd // 2 + 1) * FOX_SLOT]
            if hd % 2:
                pair = pltpu.roll(pair, FOX_HEAD_DIM, axis=1)
            cols = slice(hd * FOX_SLOT, (hd + 1) * FOX_SLOT)
            out_ref[0, :, cols] = jnp.where(slot_lane < FOX_HEAD_DIM, pair, decay[:, cols]).astype(BF16)
    vt_ref[0, 0] = (_dot_nt(wvt_ref[...], h) + vones_ref[...]).astype(BF16)
    sgt_ref[0, 0] = _sigmoid(_dot_nt(wgt_ref[...], h)).astype(BF16)


def _fox_placement():
    place = np.zeros((LANES, 2 * FOX_HEADS * FOX_SLOT), np.float32)
    k_off = FOX_HEADS * FOX_SLOT
    for h in range(FOX_HEADS):
        base = h * FOX_SLOT + FOX_HEAD_DIM
        for j in range(N_SPLIT):
            place[j * FOX_HEADS + h, base + j] = 1.0
            place[ONES_LANE, base + N_SPLIT + j] = 1.0
            place[ONES_LANE, k_off + base + j] = 1.0
            place[j * FOX_HEADS + h, k_off + base + N_SPLIT + j] = -1.0
    return jnp.asarray(place, BF16)


def _fox_proj(x, shift, scale, norm_g, w_in, b_f):
    b, s, d = x.shape
    hds, dh = FOX_HEADS, FOX_HEAD_DIM
    tm = ATT_BLOCK
    att_scale = dh ** -0.5

    wqk = jnp.concatenate([w_in[:, 0:d] * (att_scale * LOG2E), w_in[:, d:2 * d]], axis=1).astype(BF16)
    wv = w_in[:, 2 * d:3 * d].T.reshape(hds, dh, d)
    pad = FOX_VROWS - dh
    wvt = jnp.concatenate([wv, jnp.zeros((hds, pad, d), F32)], axis=1).reshape(hds * FOX_VROWS, d).astype(BF16)
    vones = jnp.concatenate([jnp.zeros((hds, dh, 1), F32), jnp.ones((hds, pad, 1), F32)],
                            axis=1).reshape(hds * FOX_VROWS, 1)
    wgt = w_in[:, 3 * d:4 * d].T.astype(BF16)
    wf = jnp.zeros((d, LANES), F32).at[:, :hds].set(w_in[:, 4 * d:]).astype(BF16)
    bf = jnp.zeros((1, LANES), F32).at[0, :hds].set(b_f)
    tri = jnp.tril(jnp.ones((tm, tm), F32)).astype(BF16)
    qw = hds * FOX_SLOT
    vrows = hds * FOX_VROWS
    return pl.pallas_call(
        _fox_proj_kernel,
        grid=(b, s // tm),
        in_specs=[
            _row_spec(tm, d), _batch_vec_spec(d), _batch_vec_spec(d), _resident((1, d)),
            _resident((d, 2 * d)), _resident((vrows, d)), _resident((vrows, 1)), _resident((d, d)),
            _resident((d, LANES)), _resident((1, LANES)), _resident((tm, tm)), _resident((LANES, 2 * qw)),
        ],
        out_specs=[_row_spec(tm, qw), _row_spec(tm, qw), _block_major_spec(vrows, tm),
                   _block_major_spec(d, tm)],
        out_shape=[
            jax.ShapeDtypeStruct((b, s, qw), BF16), jax.ShapeDtypeStruct((b, s, qw), BF16),
            jax.ShapeDtypeStruct((b, s // tm, vrows, tm), BF16), jax.ShapeDtypeStruct((b, s // tm, d, tm), BF16),
        ],
        scratch_shapes=[pltpu.VMEM((1, LANES), F32)],
        compiler_params=_params("parallel", "arbitrary"),
        name="fox_proj",
    )(x, shift, scale, norm_g.reshape(1, d), wqk, wvt, vones, wgt, wf, bf, tri, _fox_placement())


def _fox_attn_kernel(qa_ref, ka_ref, vt_ref, sgt_ref, o_ref,
                     acc_ref, m_ref, bias_ref, shift_ref, alpha_ref, s_ref, *, blk, nblk):
    dh = FOX_HEAD_DIM
    heads = range(2)
    half = blk // 2

    kp = lax.broadcasted_iota(jnp.int32, (blk, blk), 0)
    qp = lax.broadcasted_iota(jnp.int32, (blk, blk), 1)
    bias_ref[...] = jnp.where(kp <= qp, 0.0, -jnp.inf)
    acc_ref[...] = jnp.zeros_like(acc_ref)
    m_ref[...] = jnp.full_like(m_ref, -jnp.inf)

    def scores(ij, buf, diagonal):
        i, j = ij
        q = qa_ref[0, pl.ds(pl.multiple_of(i * blk, blk), blk), :]
        k = ka_ref[0, pl.ds(pl.multiple_of(j * blk, blk), blk), :]
        for hh in heads:
            slot = slice(hh * FOX_SLOT, (hh + 1) * FOX_SLOT)
            m_old = m_ref[i, hh]
            if diagonal:
                s_lo = _dot_nt(k[:half, slot], q[:, slot]) + bias_ref[:half, :]
                s_hi = _dot_nt(k[half:, slot], q[half:, slot]) + bias_ref[half:, half:]
                top_lo = jnp.max(s_lo, axis=0, keepdims=True)
                top_hi = jnp.maximum(top_lo[:, half:], jnp.max(s_hi, axis=0, keepdims=True))
                m_new = jnp.maximum(m_old, jnp.concatenate([top_lo[:, :half], top_hi], axis=1))
                s_ref[buf, hh, :half, :] = s_lo
                s_ref[buf, hh, half:, half:] = s_hi
            else:
                s = _dot_nt(k[:, slot], q[:, slot])
                m_new = jnp.maximum(m_old, jnp.max(s, axis=0, keepdims=True))
                s_ref[buf, hh] = s
            m_ref[i, hh] = m_new
            shift_ref[buf, hh] = m_new
            alpha_ref[buf, hh] = jnp.exp2(m_old - m_new)

    def values(ij, buf, diagonal):
        i, j = ij
        vt = vt_ref[0, j]
        for hh in heads:
            vth = vt[hh * FOX_VROWS:(hh + 1) * FOX_VROWS, :]
            shift = shift_ref[buf, hh]
            if diagonal:
                p_lo = jnp.exp2(s_ref[buf, hh, :half, :] - shift)
                p_hi = jnp.exp2(s_ref[buf, hh, half:, half:] - shift[:, half:])
                pv_lo = _dot(vth[:, :half], p_lo.astype(BF16))
                pv_hi = _dot(vth[:, half:], p_hi.astype(BF16))
                pv = jnp.concatenate([pv_lo[:, :half], pv_lo[:, half:] + pv_hi], axis=1)
                acc = alpha_ref[buf, hh] * acc_ref[i, hh] + pv
                rows = slice(hh * dh, (hh + 1) * dh)
                gated = acc[:dh] * (1.0 / acc[dh:dh + 1]) * sgt_ref[0, i, rows, :].astype(F32)
                o_ref[0, i, rows, :] = gated.astype(BF16)
            else:
                p = jnp.exp2(s_ref[buf, hh] - shift)
                acc_ref[i, hh] = alpha_ref[buf, hh] * acc_ref[i, hh] + _dot(vth, p.astype(BF16))

    def sweep(first, advance, n_pairs, diagonal, unroll):
        assert unroll % 2 == 0 and n_pairs > unroll

        def steps(carry, count):
            sc, va = carry
            for r in range(count):
                scores(sc, (r + 1) % 2, diagonal)
                values(va, r % 2, diagonal)
                sc, va = advance(sc), advance(va)
            return sc, va

        scores(first, 0, diagonal)
        carry = lax.fori_loop(0, (n_pairs - 1) // unroll, lambda t, c: steps(c, unroll),
                              (advance(first), first))
        tail = (n_pairs - 1) % unroll
        _, va = steps(carry, tail)
        values(va, tail % 2, diagonal)

    def next_below_diagonal(ij):
        i, j = ij
        wrap = j + 1 == i
        return jnp.where(wrap, i + 1, i), jnp.where(wrap, 0, j + 1)

    zero, one = jnp.int32(0), jnp.int32(1)
    sweep((one, zero), next_below_diagonal, nblk * (nblk - 1) // 2, False, ATT_UNROLL)
    sweep((zero, zero), lambda ij: (ij[0] + 1, ij[1] + 1), nblk, True, ATT_UNROLL)


def _fox_attention(qa, ka, vt, sgt):
    b, nblk, d, blk = sgt.shape
    s = nblk * blk
    pairs = FOX_HEADS // 2
    pair_rows = 2 * FOX_HEAD_DIM
    kern = functools.partial(_fox_attn_kernel, blk=blk, nblk=nblk)
    return pl.pallas_call(
        kern,
        grid=(b, pairs),
        in_specs=[
            pl.BlockSpec((1, s, 2 * FOX_SLOT), lambda bb, p: (bb, 0, p)),
            pl.BlockSpec((1, s, 2 * FOX_SLOT), lambda bb, p: (bb, 0, p)),
            pl.BlockSpec((1, nblk, 2 * FOX_VROWS, blk), lambda bb, p: (bb, 0, p, 0)),
            pl.BlockSpec((1, nblk, pair_rows, blk), lambda bb, p: (bb, 0, p, 0)),
        ],
        out_specs=pl.BlockSpec((1, nblk, pair_rows, blk), lambda bb, p: (bb, 0, p, 0)),
        out_shape=jax.ShapeDtypeStruct((b, nblk, d, blk), BF16),
        scratch_shapes=[
            pltpu.VMEM((nblk, 2, FOX_VROWS, blk), F32),
            pltpu.VMEM((nblk, 2, 1, blk), F32),
            pltpu.VMEM((blk, blk), F32),
            pltpu.VMEM((2, 2, 1, blk), F32),
            pltpu.VMEM((2, 2, 1, blk), F32),
            pltpu.VMEM((2, 2, blk, blk), F32),
        ],
        compiler_params=_params("parallel", "parallel"),
        name="fox_attention",
    )(qa, ka, vt, sgt)


def _chunk_prefix_rows(x, chunk):
    n, lanes = x.shape
    row = lax.broadcasted_iota(jnp.int32, x.shape, 0) & (chunk - 1)
    shift = 1
    while shift < min(chunk, SUBLANES):
        x = x + jnp.where(row >= shift, pltpu.roll(x, shift, axis=0), 0.0)
        shift *= 2
    x = x.reshape(n // chunk, chunk, lanes)
    while shift < chunk:
        moved = jnp.concatenate([jnp.zeros((n // chunk, shift, lanes), F32), x[:, :chunk - shift]], axis=1)
        x = x + moved
        shift *= 2
    return x.reshape(n, lanes)


def _hgrn_proj_kernel(x_ref, shift_ref, scale_ref, ng_ref, w_ref, lbl_ref,
                      q_ref, kk_ref, g_ref, v_ref, sg_ref, *, d, layer, chunk):
    lg = lbl_ref[...]
    e = jnp.exp(lg - jnp.max(lg, axis=0, keepdims=True))
    sm = e / jnp.sum(e, axis=0, keepdims=True)
    lb = jnp.sum(sm[0:layer + 1], axis=0, keepdims=True) - sm[0:1]

    for r0 in range(0, x_ref.shape[1], HGRN_PROJ_SUBTILE):
        rows = slice(r0, r0 + HGRN_PROJ_SUBTILE)
        h = _modulated_norm(x_ref[0, rows, :], ng_ref[...], scale_ref[0], shift_ref[0]).astype(BF16)
        q_ref[0, rows, :] = _dot(h, w_ref[:, 0:d]).astype(BF16)
        f = lb + (1.0 - lb) * _sigmoid(_dot(h, w_ref[:, d:2 * d]))
        kk_ref[0, rows, :] = (1.0 - f).astype(BF16)
        g_ref[0, rows, :] = _chunk_prefix_rows(jnp.log(f), chunk) * LOG2E
        v_ref[0, rows, :] = _silu(_dot(h, w_ref[:, 2 * d:3 * d])).astype(BF16)
        sg_ref[0, rows, :] = _silu(_dot(h, w_ref[:, 3 * d:4 * d])).astype(BF16)


def _hgrn_proj(x, shift, scale, norm_g, w_in, lb_logits, layer):
    b, s, d = x.shape
    depth = lb_logits.shape[0]
    tm = ROW_TILE // 2
    assert tm % HGRN_PROJ_SUBTILE == 0 and HGRN_PROJ_SUBTILE % HGRN_CHUNK == 0
    kern = functools.partial(_hgrn_proj_kernel, d=d, layer=layer, chunk=HGRN_CHUNK)
    bf = jax.ShapeDtypeStruct((b, s, d), BF16)
    return pl.pallas_call(
        kern,
        grid=(b, s // tm),
        in_specs=[
            _row_spec(tm, d), _batch_vec_spec(d), _batch_vec_spec(d), _resident((1, d)),
            _resident((d, 4 * d)), _resident((depth, d)),
        ],
        out_specs=[_row_spec(tm, d)] * 5,
        out_shape=[bf, bf, jax.ShapeDtypeStruct((b, s, d), F32), bf, bf],
        compiler_params=_params("parallel", "parallel"),
        name="hgrn_proj",
    )(x, shift, scale, norm_g.reshape(1, d), w_in.astype(BF16), lb_logits.astype(F32))


def _pivot_rows(g, half):
    n, lanes = g.shape
    if 2 * half >= SUBLANES:
        g3 = g.reshape(n // (2 * half), 2 * half, lanes)
        piv = jnp.broadcast_to(g3[:, half - 1:half, :], g3.shape)
        return piv.reshape(n, lanes)
    g3 = g.reshape(n // SUBLANES, SUBLANES, lanes)
    sub = lax.broadcasted_iota(jnp.int32, g3.shape, 1)
    out = None
    for grp in range(SUBLANES // (2 * half)):
        r = grp * 2 * half + half - 1
        cand = jnp.broadcast_to(g3[:, r:r + 1, :], g3.shape)
        out = cand if out is None else jnp.where(sub >= grp * 2 * half, cand, out)
    return out.reshape(n, lanes)


def _interleave_rows(lower, upper, half):
    n, lanes = lower.shape
    if half >= SUBLANES:
        lo3 = lower.reshape(n // (2 * half), 2 * half, lanes)
        up3 = upper.reshape(n // (2 * half), 2 * half, lanes)
        return jnp.concatenate([lo3[:, :half], up3[:, half:]], axis=1).reshape(n, lanes)
    row = lax.broadcasted_iota(jnp.int32, lower.shape, 0)
    return jnp.where((row & half) != 0, upper, lower)


def _hgrn_chunk_kernel(q_ref, kk_ref, g_ref, v_ref, sg_ref, gn_ref, o_ref, state_ref, level_ref,
                       *, chunk, heads):
    n_levels = chunk.bit_length() - 1

    @pl.when(pl.program_id(2) == 0)
    def _():
        state_ref[...] = jnp.zeros_like(state_ref)
        t_idx = lax.broadcasted_iota(jnp.int32, (chunk, chunk), 0)
        s_idx = lax.broadcasted_iota(jnp.int32, (chunk, chunk), 1)
        diff = t_idx ^ s_idx
        level = jnp.full((chunk, chunk), -1, jnp.int32)
        for lv in range(n_levels):
            level = level + (diff >= (1 << lv)).astype(jnp.int32)
        level_ref[...] = jnp.where(t_idx > s_idx, level, jnp.where(t_idx == s_idx, n_levels, -1))

    kdim = HGRN_EXPAND
    level = level_ref[...]
    for hh in range(heads):
        cols = slice(hh * kdim, (hh + 1) * kdim)
        qb = q_ref[0, :, cols]
        kb = kk_ref[0, :, cols]
        q = qb.astype(F32)
        kk = kb.astype(F32)
        v = v_ref[0, :, cols]
        g = g_ref[0, :, cols]

        a = jnp.where(level == n_levels, _dot_nt(qb, kb), 0.0)
        for lv in range(n_levels):
            half = 1 << lv
            e = jnp.exp2(-jnp.abs(g - _pivot_rows(g, half)))
            y = (_interleave_rows(kk, q, half) * e).astype(BF16)
            a = jnp.where(level == lv, _dot_nt(y, y), a)

        g_last = g[chunk - 1:chunk, :]
        state_t = state_ref[hh]
        o = _dot_nt((q * jnp.exp2(g)).astype(BF16), state_t.astype(BF16)) + _dot(a.astype(BF16), v)
        k_tail = (kk * jnp.exp2(g_last - g)).astype(BF16)
        state_ref[hh] = state_t * jnp.exp2(g_last) + _dot_tn(v, k_tail)

        o = o * lax.rsqrt(jnp.mean(o * o, axis=-1, keepdims=True) + EPS)
        o = (o * gn_ref[:, cols]) * sg_ref[0, :, cols].astype(F32)
        o_ref[0, :, cols] = o.astype(BF16)


def _hgrn_chunks(q, kk, g, v, sg, g_norm):
    b, s, d = q.shape
    chunk, heads = HGRN_CHUNK, HGRN_HEADS_PER_STEP
    width = heads * HGRN_EXPAND
    spec = pl.BlockSpec((1, chunk, width), lambda bb, hp, c: (bb, c, hp))
    kern = functools.partial(_hgrn_chunk_kernel, chunk=chunk, heads=heads)
    return pl.pallas_call(
        kern,
        grid=(b, d // width, s // chunk),
        in_specs=[spec] * 5 + [pl.BlockSpec((1, width), lambda bb, hp, c: (0, hp))],
        out_specs=spec,
        out_shape=jax.ShapeDtypeStruct((b, s, d), BF16),
        scratch_shapes=[pltpu.VMEM((heads, HGRN_EXPAND, HGRN_EXPAND), F32),
                        pltpu.VMEM((chunk, chunk), jnp.int32)],
        compiler_params=_params("parallel", "parallel", "arbitrary"),
        name="hgrn_chunks",
    )(q, kk, g, v, sg, g_norm.reshape(1, d).astype(F32))


def kernel(x, c, ada_w, ada_b, norm_g, ffn_w_up, ffn_w_down, fox_w_in, fox_b_f, fox_w_out,
           hgrn_w_in, hgrn_norm_g, hgrn_w_out, hgrn_lb_logits, final_norm_g):
    b, s, d = x.shape
    depth = ada_w.shape[0]
    mod = _ada_modulation(c, ada_w, ada_b).reshape(depth, b, N_SUB, 3, 1, d)
    w_up, w_down = ffn_w_up.astype(BF16), ffn_w_down.astype(BF16)
    for i in range(depth):
        shift = lambda sub: mod[i, :, sub, 0]
        scale = lambda sub: mod[i, :, sub, 1]
        gate = lambda sub: mod[i, :, sub, 2]
        x = _ffn(x, shift(0), scale(0), gate(0), norm_g[i, 0], w_up, w_down, (i, 0),
                 final_norm_g, False)
        j = i // 2
        if i % 2 == 0:
            qa, ka, vt, sgt = _fox_proj(x, shift(1), scale(1), norm_g[i, 1], fox_w_in[j], fox_b_f[j])
            mixer = (_fox_attention(qa, ka, vt, sgt), gate(1), fox_w_out[j])
        else:
            q, kk, g, v, sg = _hgrn_proj(x, shift(1), scale(1), norm_g[i, 1], hgrn_w_in[j],
                                         hgrn_lb_logits, i)
            mixer = (_hgrn_chunks(q, kk, g, v, sg, hgrn_norm_g[j]), gate(1), hgrn_w_out[j])
        x = _ffn(x, shift(2), scale(2), gate(2), norm_g[i, 2], w_up, w_down, (i, 1),
                 final_norm_g, i == depth - 1, mixer)
    return x
```

```python
import functools

import jax
import jax.numpy as jnp
import numpy as np
from jax import lax
from jax.experimental import pallas as pl
from jax.experimental.pallas import tpu as pltpu

N_SUB = 3
FOX_HEADS = 16
FOX_HEAD_DIM = 64
HGRN_EXPAND = 128
EPS = 1e-6

LANES = 128
SUBLANES = 8
VMEM_LIMIT_BYTES = 56 * 1024 * 1024

ROW_TILE = 1024
FFN_CHUNK = 256
ATT_BLOCK = 512
ATT_UNROLL = 8
HGRN_CHUNK = 128
HGRN_HEADS_PER_STEP = 8
HGRN_PROJ_SUBTILE = 256
N_SPLIT = 3

BF16 = jnp.bfloat16
F32 = jnp.float32


def _params(*sem):
    return pltpu.CompilerParams(dimension_semantics=sem, vmem_limit_bytes=VMEM_LIMIT_BYTES)


def _resident(shape):
    zeros = (0,) * len(shape)
    return pl.BlockSpec(shape, lambda *_: zeros, pipeline_mode=pl.Buffered(1))


def _row_spec(tm, width):
    return pl.BlockSpec((1, tm, width), lambda b, i: (b, i, 0))


def _block_major_spec(features, tm):
    return pl.BlockSpec((1, 1, features, tm), lambda b, i: (b, i, 0, 0))


def _batch_vec_spec(width):
    return pl.BlockSpec((1, 1, width), lambda b, i: (b, 0, 0))


def _dot(a, b):
    return jnp.dot(a, b, preferred_element_type=F32)


def _dot_nt(a, b):
    return lax.dot_general(a, b, (((1,), (1,)), ((), ())), preferred_element_type=F32)


def _dot_tn(a, b):
    return lax.dot_general(a, b, (((0,), (0,)), ((), ())), preferred_element_type=F32)


def _sigmoid(x):
    return 1.0 / (1.0 + jnp.exp(-x))


def _silu(x):
    return x * _sigmoid(x)


def _modulated_norm(x, norm_g, scale, shift):
    y = x * lax.rsqrt(jnp.mean(x * x, axis=-1, keepdims=True) + EPS)
    return (y * norm_g) * (1.0 + scale) + shift


def _split_bf16(x, n):
    pieces = []
    rest = x
    for _ in range(n):
        p = rest.astype(BF16)
        pieces.append(p)
        rest = rest - p.astype(F32)
    return pieces


def _ada_kernel(c_ref, w_ref, b_ref, o_ref):
    cond = _silu(c_ref[...])
    y = jnp.dot(cond, w_ref[0], preferred_element_type=F32,
                precision=lax.Precision.HIGHEST)
    o_ref[0] = y + b_ref[0]


def _ada_modulation(c, ada_w, ada_b):
    depth, d, n = ada_w.shape
    b = c.shape[0]
    bp = -(-b // SUBLANES) * SUBLANES
    tn = n // 9 if n % (9 * LANES) == 0 else n
    c_pad = jnp.zeros((bp, d), F32).at[:b].set(c)
    out = pl.pallas_call(
        _ada_kernel,
        grid=(depth, n // tn),
        in_specs=[
            pl.BlockSpec((bp, d), lambda i, j: (0, 0)),
            pl.BlockSpec((1, d, tn), lambda i, j: (i, 0, j)),
            pl.BlockSpec((1, 1, tn), lambda i, j: (i, 0, j)),
        ],
        out_specs=pl.BlockSpec((1, bp, tn), lambda i, j: (i, 0, j)),
        out_shape=jax.ShapeDtypeStruct((depth, bp, n), F32),
        compiler_params=_params("parallel", "parallel"),
        name="ada_modulation",
    )(c_pad, ada_w, ada_b.reshape(depth, 1, n))
    return out[:, :b]


def _ffn_kernel(*refs, tf, mixer, final_norm):
    if mixer is None:
        x_ref, shift_ref, scale_ref, gate_ref, ng_ref, wup_ref, wdn_ref, fg_ref = refs[:8]
    else:
        (x_ref, mix_ref, mgate_ref, wout_ref,
         shift_ref, scale_ref, gate_ref, ng_ref, wup_ref, wdn_ref, fg_ref) = refs[:11]
    o_ref, h_ref, acc_ref = refs[-3:]
    wup_ref, wdn_ref = wup_ref.at[0, 0], wdn_ref.at[0, 0]
    hidden = wdn_ref.shape[0]

    x = x_ref[0]
    if mixer == "rows":
        x = x + mgate_ref[0] * _dot(mix_ref[0], wout_ref[...])
    elif mixer == "features":
        parts = [_dot_tn(mix_ref[0, blk], wout_ref[...]) for blk in range(mix_ref.shape[1])]
        x = x + mgate_ref[0] * jnp.concatenate(parts, axis=0)
    o_ref[0] = x
    h_ref[...] = _modulated_norm(x, ng_ref[...], scale_ref[0], shift_ref[0]).astype(BF16)

    for f in range(0, hidden, tf):
        a = _dot(h_ref[...], wup_ref[:, f:f + tf])
        b = _dot(h_ref[...], wup_ref[:, hidden + f:hidden + f + tf])
        down = _dot((_silu(a) * b).astype(BF16), wdn_ref[f:f + tf, :])
        if f == 0:
            acc_ref[...] = down
        else:
            acc_ref[...] += down

    y = o_ref[0] + (0.5 * gate_ref[0]) * acc_ref[...]
    if final_norm:
        y = (y * lax.rsqrt(jnp.mean(y * y, axis=-1, keepdims=True) + EPS)) * fg_ref[...]
    o_ref[0] = y


def _ffn(x, shift, scale, gate, norm_g, w_up, w_down, which, final_g, final_norm, mixer=None):
    b, s, d = x.shape
    f = w_down.shape[2]
    stack_index = tuple(which) + (0, 0)
    stack_spec = lambda rows, cols: pl.BlockSpec((1, 1, rows, cols), lambda bb, i: stack_index,
                                                 pipeline_mode=pl.Buffered(1))
    tf = FFN_CHUNK
    assert f % tf == 0
    tm = ROW_TILE
    operands, specs, mode = [x], [_row_spec(tm, d)], None
    if mixer is not None:
        o, mgate, w_out = mixer
        if o.ndim == 4:
            mode, blk = "features", o.shape[3]
            o_spec = pl.BlockSpec((1, tm // blk, d, blk), lambda bb, i: (bb, i, 0, 0))
        else:
            mode, o_spec = "rows", _row_spec(tm, d)
        operands += [o, mgate, w_out.astype(BF16)]
        specs += [o_spec, _batch_vec_spec(d), _resident((d, d))]
    operands += [shift, scale, gate, norm_g.reshape(1, d), w_up, w_down, final_g.reshape(1, d)]
    specs += [_batch_vec_spec(d), _batch_vec_spec(d), _batch_vec_spec(d), _resident((1, d)),
              stack_spec(d, 2 * f), stack_spec(f, d), _resident((1, d))]
    kern = functools.partial(_ffn_kernel, tf=tf, mixer=mode, final_norm=final_norm)
    return pl.pallas_call(
        kern,
        grid=(b, s // tm),
        in_specs=specs,
        out_specs=_row_spec(tm, d),
        out_shape=jax.ShapeDtypeStruct((b, s, d), F32),
        scratch_shapes=[pltpu.VMEM((tm, d), BF16), pltpu.VMEM((tm, d), F32)],
        compiler_params=_params("parallel", "parallel"),
        name="ffn" + ("" if mode is None else "_" + mode) + ("_final" if final_norm else ""),
    )(*operands)


FOX_SLOT = 2 * FOX_HEAD_DIM
ONES_LANE = N_SPLIT * FOX_HEADS
FOX_VROWS = FOX_HEAD_DIM + 16
LOG2E = 1.4426950408889634


def _fox_proj_kernel(x_ref, shift_ref, scale_ref, ng_ref, wqk_ref, wvt_ref, vones_ref, wgt_ref,
                     wf_ref, bf_ref, tri_ref, place_ref, qa_ref, ka_ref, vt_ref, sgt_ref, carry_ref):
    @pl.when(pl.program_id(1) == 0)
    def _():
        carry_ref[...] = jnp.zeros_like(carry_ref)

    h = _modulated_norm(x_ref[0], ng_ref[...], scale_ref[0], shift_ref[0]).astype(BF16)
    z = _dot(h, wf_ref[...]) + bf_ref[...]
    logf = jnp.minimum(z, 0.0) - jnp.log(1.0 + jnp.exp(-jnp.abs(z)))
    pieces = jnp.concatenate(_split_bf16(logf, N_SPLIT), axis=1)
    part = _dot(tri_ref[...], pieces)
    cum = carry_ref[...] + ((part[:, :LANES] + part[:, LANES:2 * LANES]) + part[:, 2 * LANES:])
    carry_ref[...] = cum[-1:, :]

    lane = lax.broadcasted_iota(jnp.int32, cum.shape, 1)
    operand = jnp.where(lane == ONES_LANE, 1.0, 0.0).astype(F32)
    for j, p in enumerate(_split_bf16(cum * LOG2E, N_SPLIT)):
        pf = p.astype(F32)
        shifted = pf if j == 0 else pltpu.roll(pf, j * FOX_HEADS, axis=1)
        operand = jnp.where((lane >= j * FOX_HEADS) & (lane < (j + 1) * FOX_HEADS), shifted, operand)
    operand = operand.astype(BF16)
    d = wqk_ref.shape[0]
    slot_lane = lax.broadcasted_iota(jnp.int32, (operand.shape[0], FOX_SLOT), 1)
    width = FOX_HEADS * FOX_SLOT

    def project(w_cols, place_cols):
        packed = _dot(h, wqk_ref[:, w_cols:w_cols + d])
        return packed, _dot(operand, place_ref[:, place_cols:place_cols + width])

    def store_slots(out_ref, packed, decay):
        for hd in range(FOX_HEADS):
            pair = packed[:, (hd // 2) * FOX_SLOT:(hd // 2 + 1) * FOX_SLOT]
            if hd % 2:
                pair = pltpu.roll(pair, FOX_HEAD_DIM, axis=1)
            cols = slice(hd * FOX_SLOT, (hd + 1) * FOX_SLOT)
            out_ref[0, :, cols] = jnp.where(slot_lane < FOX_HEAD_DIM, pair, decay[:, cols]).astype(BF16)

    q_parts = project(0, 0)
    k_parts = project(d, width)
    store_slots(qa_ref, *q_parts)
    vt = _dot_nt(wvt_ref[...], h)
    store_slots(ka_ref, *k_parts)
    zg = _dot_nt(wgt_ref[...], h)
    vt_ref[0, 0] = (vt + vones_ref[...]).astype(BF16)
    sgt_ref[0, 0] = _sigmoid(zg).astype(BF16)


def _fox_placement():
    place = np.zeros((LANES, 2 * FOX_HEADS * FOX_SLOT), np.float32)
    k_off = FOX_HEADS * FOX_SLOT
    for h in range(FOX_HEADS):
        base = h * FOX_SLOT + FOX_HEAD_DIM
        for j in range(N_SPLIT):
            place[j * FOX_HEADS + h, base + j] = 1.0
            place[ONES_LANE, base + N_SPLIT + j] = 1.0
            place[ONES_LANE, k_off + base + j] = 1.0
            place[j * FOX_HEADS + h, k_off + base + N_SPLIT + j] = -1.0
    return jnp.asarray(place, BF16)


def _fox_proj(x, shift, scale, norm_g, w_in, b_f):
    b, s, d = x.shape
    hds, dh = FOX_HEADS, FOX_HEAD_DIM
    tm = ATT_BLOCK
    att_scale = dh ** -0.5

    wqk = jnp.concatenate([w_in[:, 0:d] * (att_scale * LOG2E), w_in[:, d:2 * d]], axis=1).astype(BF16)
    wv = w_in[:, 2 * d:3 * d].T.reshape(hds, dh, d)
    pad = FOX_VROWS - dh
    wvt = jnp.concatenate([wv, jnp.zeros((hds, pad, d), F32)], axis=1).reshape(hds * FOX_VROWS, d).astype(BF16)
    vones = jnp.concatenate([jnp.zeros((hds, dh, 1), F32), jnp.ones((hds, pad, 1), F32)],
                            axis=1).reshape(hds * FOX_VROWS, 1)
    wgt = w_in[:, 3 * d:4 * d].T.astype(BF16)
    wf = jnp.zeros((d, LANES), F32).at[:, :hds].set(w_in[:, 4 * d:]).astype(BF16)
    bf = jnp.zeros((1, LANES), F32).at[0, :hds].set(b_f)
    tri = jnp.tril(jnp.ones((tm, tm), F32)).astype(BF16)
    qw = hds * FOX_SLOT
    vrows = hds * FOX_VROWS
    return pl.pallas_call(
        _fox_proj_kernel,
        grid=(b, s // tm),
        in_specs=[
            _row_spec(tm, d), _batch_vec_spec(d), _batch_vec_spec(d), _resident((1, d)),
            _resident((d, 2 * d)), _resident((vrows, d)), _resident((vrows, 1)), _resident((d, d)),
            _resident((d, LANES)), _resident((1, LANES)), _resident((tm, tm)), _resident((LANES, 2 * qw)),
        ],
        out_specs=[_row_spec(tm, qw), _row_spec(tm, qw), _block_major_spec(vrows, tm),
                   _block_major_spec(d, tm)],
        out_shape=[
            jax.ShapeDtypeStruct((b, s, qw), BF16), jax.ShapeDtypeStruct((b, s, qw), BF16),
            jax.ShapeDtypeStruct((b, s // tm, vrows, tm), BF16), jax.ShapeDtypeStruct((b, s // tm, d, tm), BF16),
        ],
        scratch_shapes=[pltpu.VMEM((1, LANES), F32)],
        compiler_params=_params("parallel", "arbitrary"),
        name="fox_proj",
    )(x, shift, scale, norm_g.reshape(1, d), wqk, wvt, vones, wgt, wf, bf, tri, _fox_placement())


def _fox_attn_kernel(qa_ref, ka_ref, vt_ref, sgt_ref, o_ref,
                     acc_ref, m_ref, bias_ref, shift_ref, alpha_ref, s_ref, *, blk, nblk):
    dh = FOX_HEAD_DIM
    heads = range(2)
    half = blk // 2

    kp = lax.broadcasted_iota(jnp.int32, (blk, blk), 0)
    qp = lax.broadcasted_iota(jnp.int32, (blk, blk), 1)
    bias_ref[...] = jnp.where(kp <= qp, 0.0, -jnp.inf)
    acc_ref[...] = jnp.zeros_like(acc_ref)
    m_ref[...] = jnp.full_like(m_ref, -jnp.inf)

    def scores(ij, buf, diagonal, which):
        i, j = ij
        q = qa_ref[0, pl.ds(pl.multiple_of(i * blk, blk), blk), :]
        k = ka_ref[0, pl.ds(pl.multiple_of(j * blk, blk), blk), :]
        for hh in which:
            slot = slice(hh * FOX_SLOT, (hh + 1) * FOX_SLOT)
            m_old = m_ref[i, hh]
            if diagonal:
                s_lo = _dot_nt(k[:half, slot], q[:, slot]) + bias_ref[:half, :]
                s_hi = _dot_nt(k[half:, slot], q[half:, slot]) + bias_ref[half:, half:]
                top_lo = jnp.max(s_lo, axis=0, keepdims=True)
                top_hi = jnp.maximum(top_lo[:, half:], jnp.max(s_hi, axis=0, keepdims=True))
                m_new = jnp.maximum(m_old, jnp.concatenate([top_lo[:, :half], top_hi], axis=1))
                s_ref[buf, hh, :half, :] = s_lo
                s_ref[buf, hh, half:, half:] = s_hi
            else:
                s = _dot_nt(k[:, slot], q[:, slot])
                m_new = jnp.maximum(m_old, jnp.max(s, axis=0, keepdims=True))
                s_ref[buf, hh] = s
            m_ref[i, hh] = m_new
            shift_ref[buf, hh] = m_new
            alpha_ref[buf, hh] = jnp.exp2(m_old - m_new)

    def values(ij, buf, diagonal, which):
        i, j = ij
        vt = vt_ref[0, j]
        for hh in which:
            vth = vt[hh * FOX_VROWS:(hh + 1) * FOX_VROWS, :]
            shift = shift_ref[buf, hh]
            if diagonal:
                p_lo = jnp.exp2(s_ref[buf, hh, :half, :] - shift)
                p_hi = jnp.exp2(s_ref[buf, hh, half:, half:] - shift[:, half:])
                pv_lo = _dot(vth[:, :half], p_lo.astype(BF16))
                pv_hi = _dot(vth[:, half:], p_hi.astype(BF16))
                pv = jnp.concatenate([pv_lo[:, :half], pv_lo[:, half:] + pv_hi], axis=1)
                acc = alpha_ref[buf, hh] * acc_ref[i, hh] + pv
                rows = slice(hh * dh, (hh + 1) * dh)
                gated = acc[:dh] * (1.0 / acc[dh:dh + 1]) * sgt_ref[0, i, rows, :].astype(F32)
                o_ref[0, i, rows, :] = gated.astype(BF16)
            else:
                p = jnp.exp2(s_ref[buf, hh] - shift)
                acc_ref[i, hh] = alpha_ref[buf, hh] * acc_ref[i, hh] + _dot(vth, p.astype(BF16))

    def sweep(first, advance, n_pairs, diagonal, unroll):
        assert unroll % 2 == 0 and n_pairs > unroll

        def steps(carry, count):
            sc, va = carry
            for r in range(count):
                for hh in heads:
                    scores(sc, (r + 1) % 2, diagonal, (hh,))
                    values(va, r % 2, diagonal, (hh,))
                sc, va = advance(sc), advance(va)
            return sc, va

        scores(first, 0, diagonal, heads)
        carry = lax.fori_loop(0, (n_pairs - 1) // unroll, lambda t, c: steps(c, unroll),
                              (advance(first), first))
        tail = (n_pairs - 1) % unroll
        _, va = steps(carry, tail)
        values(va, tail % 2, diagonal, heads)

    def next_below_diagonal(ij):
        i, j = ij
        wrap = j + 1 == i
        return jnp.where(wrap, i + 1, i), jnp.where(wrap, 0, j + 1)

    zero, one = jnp.int32(0), jnp.int32(1)
    sweep((one, zero), next_below_diagonal, nblk * (nblk - 1) // 2, False, ATT_UNROLL)
    sweep((zero, zero), lambda ij: (ij[0] + 1, ij[1] + 1), nblk, True, ATT_UNROLL)


def _fox_attention(qa, ka, vt, sgt):
    b, nblk, d, blk = sgt.shape
    s = nblk * blk
    pairs = FOX_HEADS // 2
    pair_rows = 2 * FOX_HEAD_DIM
    kern = functools.partial(_fox_attn_kernel, blk=blk, nblk=nblk)
    return pl.pallas_call(
        kern,
        grid=(b, pairs),
        in_specs=[
            pl.BlockSpec((1, s, 2 * FOX_SLOT), lambda bb, p: (bb, 0, p)),
            pl.BlockSpec((1, s, 2 * FOX_SLOT), lambda bb, p: (bb, 0, p)),
            pl.BlockSpec((1, nblk, 2 * FOX_VROWS, blk), lambda bb, p: (bb, 0, p, 0)),
            pl.BlockSpec((1, nblk, pair_rows, blk), lambda bb, p: (bb, 0, p, 0)),
        ],
        out_specs=pl.BlockSpec((1, nblk, pair_rows, blk), lambda bb, p: (bb, 0, p, 0)),
        out_shape=jax.ShapeDtypeStruct((b, nblk, d, blk), BF16),
        scratch_shapes=[
            pltpu.VMEM((nblk, 2, FOX_VROWS, blk), F32),
            pltpu.VMEM((nblk, 2, 1, blk), F32),
            pltpu.VMEM((blk, blk), F32),
            pltpu.VMEM((2, 2, 1, blk), F32),
            pltpu.VMEM((2, 2, 1, blk), F32),
            pltpu.VMEM((2, 2, blk, blk), F32),
        ],
        compiler_params=_params("parallel", "parallel"),
        name="fox_attention",
    )(qa, ka, vt, sgt)


def _chunk_prefix_rows(x, chunk):
    n, lanes = x.shape
    row = lax.broadcasted_iota(jnp.int32, x.shape, 0) & (chunk - 1)
    shift = 1
    while shift < min(chunk, SUBLANES):
        x = x + jnp.where(row >= shift, pltpu.roll(x, shift, axis=0), 0.0)
        shift *= 2
    x = x.reshape(n // chunk, chunk, lanes)
    while shift < chunk:
        moved = jnp.concatenate([jnp.zeros((n // chunk, shift, lanes), F32), x[:, :chunk - shift]], axis=1)
        x = x + moved
        shift *= 2
    return x.reshape(n, lanes)


def _hgrn_proj_kernel(x_ref, shift_ref, scale_ref, ng_ref, w_ref, lbl_ref,
                      q_ref, kk_ref, g_ref, v_ref, sg_ref, *, d, layer, chunk):
    lg = lbl_ref[...]
    e = jnp.exp(lg - jnp.max(lg, axis=0, keepdims=True))
    sm = e / jnp.sum(e, axis=0, keepdims=True)
    lb = jnp.sum(sm[0:layer + 1], axis=0, keepdims=True) - sm[0:1]

    for r0 in range(0, x_ref.shape[1], HGRN_PROJ_SUBTILE):
        rows = slice(r0, r0 + HGRN_PROJ_SUBTILE)
        h = _modulated_norm(x_ref[0, rows, :], ng_ref[...], scale_ref[0], shift_ref[0]).astype(BF16)
        zq = _dot(h, w_ref[:, 0:d])
        zf = _dot(h, w_ref[:, d:2 * d])
        q_ref[0, rows, :] = zq.astype(BF16)
        zv = _dot(h, w_ref[:, 2 * d:3 * d])
        f = lb + (1.0 - lb) * _sigmoid(zf)
        kk_ref[0, rows, :] = (1.0 - f).astype(BF16)
        g_ref[0, rows, :] = _chunk_prefix_rows(jnp.log(f), chunk) * LOG2E
        zg = _dot(h, w_ref[:, 3 * d:4 * d])
        v_ref[0, rows, :] = _silu(zv).astype(BF16)
        sg_ref[0, rows, :] = _silu(zg).astype(BF16)


def _hgrn_proj(x, shift, scale, norm_g, w_in, lb_logits, layer):
    b, s, d = x.shape
    depth = lb_logits.shape[0]
    tm = ROW_TILE // 2
    assert tm % HGRN_PROJ_SUBTILE == 0 and HGRN_PROJ_SUBTILE % HGRN_CHUNK == 0
    kern = functools.partial(_hgrn_proj_kernel, d=d, layer=layer, chunk=HGRN_CHUNK)
    bf = jax.ShapeDtypeStruct((b, s, d), BF16)
    return pl.pallas_call(
        kern,
        grid=(b, s // tm),
        in_specs=[
            _row_spec(tm, d), _batch_vec_spec(d), _batch_vec_spec(d), _resident((1, d)),
            _resident((d, 4 * d)), _resident((depth, d)),
        ],
        out_specs=[_row_spec(tm, d)] * 5,
        out_shape=[bf, bf, jax.ShapeDtypeStruct((b, s, d), F32), bf, bf],
        compiler_params=_params("parallel", "parallel"),
        name="hgrn_proj",
    )(x, shift, scale, norm_g.reshape(1, d), w_in.astype(BF16), lb_logits.astype(F32))


def _pivot_rows(g, half):
    n, lanes = g.shape
    if 2 * half >= SUBLANES:
        g3 = g.reshape(n // (2 * half), 2 * half, lanes)
        piv = jnp.broadcast_to(g3[:, half - 1:half, :], g3.shape)
        return piv.reshape(n, lanes)
    g3 = g.reshape(n // SUBLANES, SUBLANES, lanes)
    sub = lax.broadcasted_iota(jnp.int32, g3.shape, 1)
    out = None
    for grp in range(SUBLANES // (2 * half)):
        r = grp * 2 * half + half - 1
        cand = jnp.broadcast_to(g3[:, r:r + 1, :], g3.shape)
        out = cand if out is None else jnp.where(sub >= grp * 2 * half, cand, out)
    return out.reshape(n, lanes)


def _interleave_rows(lower, upper, half):
    n, lanes = lower.shape
    if half >= SUBLANES:
        lo3 = lower.reshape(n // (2 * half), 2 * half, lanes)
        up3 = upper.reshape(n // (2 * half), 2 * half, lanes)
        return jnp.concatenate([lo3[:, :half], up3[:, half:]], axis=1).reshape(n, lanes)
    row = lax.broadcasted_iota(jnp.int32, lower.shape, 0)
    return jnp.where((row & half) != 0, upper, lower)


def _hgrn_chunk_kernel(q_ref, kk_ref, g_ref, v_ref, sg_ref, gn_ref, o_ref, state_ref, level_ref,
                       *, chunk, heads):
    n_levels = chunk.bit_length() - 1

    @pl.when(pl.program_id(2) == 0)
    def _():
        state_ref[...] = jnp.zeros_like(state_ref)
        t_idx = lax.broadcasted_iota(jnp.int32, (chunk, chunk), 0)
        s_idx = lax.broadcasted_iota(jnp.int32, (chunk, chunk), 1)
        diff = t_idx ^ s_idx
        level = jnp.full((chunk, chunk), -1, jnp.int32)
        for lv in range(n_levels):
            level = level + (diff >= (1 << lv)).astype(jnp.int32)
        level_ref[...] = jnp.where(t_idx > s_idx, level, jnp.where(t_idx == s_idx, n_levels, -1))

    kdim = HGRN_EXPAND
    level = level_ref[...]
    at_level = [level == lv for lv in range(n_levels + 1)]

    def within_chunk(hh):
        cols = slice(hh * kdim, (hh + 1) * kdim)
        qb = q_ref[0, :, cols]
        kb = kk_ref[0, :, cols]
        q = qb.astype(F32)
        kk = kb.astype(F32)
        g = g_ref[0, :, cols]
        a = jnp.where(at_level[n_levels], _dot_nt(qb, kb), 0.0)
        for lv in range(n_levels):
            half = 1 << lv
            e = jnp.exp2(-jnp.abs(g - _pivot_rows(g, half)))
            y = (_interleave_rows(kk, q, half) * e).astype(BF16)
            a = jnp.where(at_level[lv], _dot_nt(y, y), a)
        return a.astype(BF16)

    def across_chunks(hh, a):
        cols = slice(hh * kdim, (hh + 1) * kdim)
        q = q_ref[0, :, cols].astype(F32)
        kk = kk_ref[0, :, cols].astype(F32)
        v = v_ref[0, :, cols]
        g = g_ref[0, :, cols]
        g_last = g[chunk - 1:chunk, :]
        state_t = state_ref[hh]
        o = _dot_nt((q * jnp.exp2(g)).astype(BF16), state_t.astype(BF16)) + _dot(a, v)
        k_tail = (kk * jnp.exp2(g_last - g)).astype(BF16)
        state_ref[hh] = state_t * jnp.exp2(g_last) + _dot_tn(v, k_tail)

        o = o * lax.rsqrt(jnp.mean(o * o, axis=-1, keepdims=True) + EPS)
        o = (o * gn_ref[:, cols]) * sg_ref[0, :, cols].astype(F32)
        o_ref[0, :, cols] = o.astype(BF16)

    pending = None
    for hh in range(heads):
        a = within_chunk(hh)
        if pending is not None:
            across_chunks(*pending)
        pending = (hh, a)
    across_chunks(*pending)


def _hgrn_chunks(q, kk, g, v, sg, g_norm):
    b, s, d = q.shape
    chunk, heads = HGRN_CHUNK, HGRN_HEADS_PER_STEP
    width = heads * HGRN_EXPAND
    spec = pl.BlockSpec((1, chunk, width), lambda bb, hp, c: (bb, c, hp))
    kern = functools.partial(_hgrn_chunk_kernel, chunk=chunk, heads=heads)
    return pl.pallas_call(
        kern,
        grid=(b, d // width, s // chunk),
        in_specs=[spec] * 5 + [pl.BlockSpec((1, width), lambda bb, hp, c: (0, hp))],
        out_specs=spec,
        out_shape=jax.ShapeDtypeStruct((b, s, d), BF16),
        scratch_shapes=[pltpu.VMEM((heads, HGRN_EXPAND, HGRN_EXPAND), F32),
                        pltpu.VMEM((chunk, chunk), jnp.int32)],
        compiler_params=_params("parallel", "parallel", "arbitrary"),
        name="hgrn_chunks",
    )(q, kk, g, v, sg, g_norm.reshape(1, d).astype(F32))


def kernel(x, c, ada_w, ada_b, norm_g, ffn_w_up, ffn_w_down, fox_w_in, fox_b_f, fox_w_out,
           hgrn_w_in, hgrn_norm_g, hgrn_w_out, hgrn_lb_logits, final_norm_g):
    b, s, d = x.shape
    depth = ada_w.shape[0]
    mod = _ada_modulation(c, ada_w, ada_b).reshape(depth, b, N_SUB, 3, 1, d)
    w_up, w_down = ffn_w_up.astype(BF16), ffn_w_down.astype(BF16)
    for i in range(depth):
        shift = lambda sub: mod[i, :, sub, 0]
        scale = lambda sub: mod[i, :, sub, 1]
        gate = lambda sub: mod[i, :, sub, 2]
        x = _ffn(x, shift(0), scale(0), gate(0), norm_g[i, 0], w_up, w_down, (i, 0),
                 final_norm_g, False)
        j = i // 2
        if i % 2 == 0:
            qa, ka, vt, sgt = _fox_proj(x, shift(1), scale(1), norm_g[i, 1], fox_w_in[j], fox_b_f[j])
            mixer = (_fox_attention(qa, ka, vt, sgt), gate(1), fox_w_out[j])
        else:
            q, kk, g, v, sg = _hgrn_proj(x, shift(1), scale(1), norm_g[i, 1], hgrn_w_in[j],
                                         hgrn_lb_logits, i)
            mixer = (_hgrn_chunks(q, kk, g, v, sg, hgrn_norm_g[j]), gate(1), hgrn_w_out[j])
        x = _ffn(x, shift(2), scale(2), gate(2), norm_g[i, 2], w_up, w_down, (i, 1),
                 final_norm_g, i == depth - 1, mixer)
    return x
```

```python
import functools

import jax
import jax.numpy as jnp
import numpy as np
from jax import lax
from jax.experimental import pallas as pl
from jax.experimental.pallas import tpu as pltpu

N_SUB = 3
FOX_HEADS = 16
FOX_HEAD_DIM = 64
HGRN_EXPAND = 128
EPS = 1e-6

LANES = 128
SUBLANES = 8
VMEM_LIMIT_BYTES = 56 * 1024 * 1024

ROW_TILE = 1024
FFN_CHUNK = 256
ATT_BLOCK = 512
ATT_UNROLL = 8
HGRN_CHUNK = 128
HGRN_HEADS_PER_STEP = 8
HGRN_PROJ_SUBTILE = 256
HGRN_PROJ_COLS = 256
N_SPLIT = 3

BF16 = jnp.bfloat16
F32 = jnp.float32


def _params(*sem):
    return pltpu.CompilerParams(dimension_semantics=sem, vmem_limit_bytes=VMEM_LIMIT_BYTES)


def _resident(shape):
    zeros = (0,) * len(shape)
    return pl.BlockSpec(shape, lambda *_: zeros, pipeline_mode=pl.Buffered(1))


def _row_spec(tm, width):
    return pl.BlockSpec((1, tm, width), lambda b, i: (b, i, 0))


def _block_major_spec(features, tm):
    return pl.BlockSpec((1, 1, features, tm), lambda b, i: (b, i, 0, 0))


def _batch_vec_spec(width):
    return pl.BlockSpec((1, 1, width), lambda b, i: (b, 0, 0))


def _dot(a, b):
    return jnp.dot(a, b, preferred_element_type=F32)


def _dot_nt(a, b):
    return lax.dot_general(a, b, (((1,), (1,)), ((), ())), preferred_element_type=F32)


def _dot_tn(a, b):
    return lax.dot_general(a, b, (((0,), (0,)), ((), ())), preferred_element_type=F32)


def _sigmoid(x):
    return 1.0 / (1.0 + jnp.exp(-x))


def _silu(x):
    return x * _sigmoid(x)


def _modulated_norm(x, norm_g, scale, shift):
    y = x * lax.rsqrt(jnp.mean(x * x, axis=-1, keepdims=True) + EPS)
    return (y * norm_g) * (1.0 + scale) + shift


def _split_bf16(x, n):
    pieces = []
    rest = x
    for _ in range(n):
        p = rest.astype(BF16)
        pieces.append(p)
        rest = rest - p.astype(F32)
    return pieces


def _ada_kernel(c_ref, w_ref, b_ref, o_ref):
    cond = _silu(c_ref[...])
    y = jnp.dot(cond, w_ref[0], preferred_element_type=F32,
                precision=lax.Precision.HIGHEST)
    o_ref[0] = y + b_ref[0]


def _ada_modulation(c, ada_w, ada_b):
    depth, d, n = ada_w.shape
    b = c.shape[0]
    bp = -(-b // SUBLANES) * SUBLANES
    tn = n // 9 if n % (9 * LANES) == 0 else n
    c_pad = jnp.zeros((bp, d), F32).at[:b].set(c)
    out = pl.pallas_call(
        _ada_kernel,
        grid=(depth, n // tn),
        in_specs=[
            pl.BlockSpec((bp, d), lambda i, j: (0, 0)),
            pl.BlockSpec((1, d, tn), lambda i, j: (i, 0, j)),
            pl.BlockSpec((1, 1, tn), lambda i, j: (i, 0, j)),
        ],
        out_specs=pl.BlockSpec((1, bp, tn), lambda i, j: (i, 0, j)),
        out_shape=jax.ShapeDtypeStruct((depth, bp, n), F32),
        compiler_params=_params("parallel", "parallel"),
        name="ada_modulation",
    )(c_pad, ada_w, ada_b.reshape(depth, 1, n))
    return out[:, :b]


def _ffn_kernel(*refs, tf, mixer, final_norm):
    if mixer is None:
        x_ref, shift_ref, scale_ref, gate_ref, ng_ref, wup_ref, wdn_ref, fg_ref = refs[:8]
    else:
        (x_ref, mix_ref, mgate_ref, wout_ref,
         shift_ref, scale_ref, gate_ref, ng_ref, wup_ref, wdn_ref, fg_ref) = refs[:11]
    o_ref, h_ref, acc_ref = refs[-3:]
    wup_ref, wdn_ref = wup_ref.at[0, 0], wdn_ref.at[0, 0]
    hidden = wdn_ref.shape[0]

    x = x_ref[0]
    if mixer == "rows":
        x = x + mgate_ref[0] * _dot(mix_ref[0], wout_ref[...])
    elif mixer == "features":
        parts = [_dot_tn(mix_ref[0, blk], wout_ref[...]) for blk in range(mix_ref.shape[1])]
        x = x + mgate_ref[0] * jnp.concatenate(parts, axis=0)
    o_ref[0] = x
    h_ref[...] = _modulated_norm(x, ng_ref[...], scale_ref[0], shift_ref[0]).astype(BF16)

    def up(f):
        return (_dot(h_ref[...], wup_ref[:, f:f + tf]),
                _dot(h_ref[...], wup_ref[:, hidden + f:hidden + f + tf]))

    ab = up(0)
    for f in range(0, hidden, tf):
        nxt = up(f + tf) if f + tf < hidden else None
        down = _dot((_silu(ab[0]) * ab[1]).astype(BF16), wdn_ref[f:f + tf, :])
        if f == 0:
            acc_ref[...] = down
        else:
            acc_ref[...] += down
        ab = nxt

    y = o_ref[0] + (0.5 * gate_ref[0]) * acc_ref[...]
    if final_norm:
        y = (y * lax.rsqrt(jnp.mean(y * y, axis=-1, keepdims=True) + EPS)) * fg_ref[...]
    o_ref[0] = y


def _ffn(x, shift, scale, gate, norm_g, w_up, w_down, which, final_g, final_norm, mixer=None):
    b, s, d = x.shape
    f = w_down.shape[2]
    stack_index = tuple(which) + (0, 0)
    stack_spec = lambda rows, cols: pl.BlockSpec((1, 1, rows, cols), lambda bb, i: stack_index,
                                                 pipeline_mode=pl.Buffered(1))
    tf = FFN_CHUNK
    assert f % tf == 0
    tm = ROW_TILE
    operands, specs, mode = [x], [_row_spec(tm, d)], None
    if mixer is not None:
        o, mgate, w_out = mixer
        if o.ndim == 4:
            mode, blk = "features", o.shape[3]
            o_spec = pl.BlockSpec((1, tm // blk, d, blk), lambda bb, i: (bb, i, 0, 0))
        else:
            mode, o_spec = "rows", _row_spec(tm, d)
        operands += [o, mgate, w_out.astype(BF16)]
        specs += [o_spec, _batch_vec_spec(d), _resident((d, d))]
    operands += [shift, scale, gate, norm_g.reshape(1, d), w_up, w_down, final_g.reshape(1, d)]
    specs += [_batch_vec_spec(d), _batch_vec_spec(d), _batch_vec_spec(d), _resident((1, d)),
              stack_spec(d, 2 * f), stack_spec(f, d), _resident((1, d))]
    kern = functools.partial(_ffn_kernel, tf=tf, mixer=mode, final_norm=final_norm)
    return pl.pallas_call(
        kern,
        grid=(b, s // tm),
        in_specs=specs,
        out_specs=_row_spec(tm, d),
        out_shape=jax.ShapeDtypeStruct((b, s, d), F32),
        scratch_shapes=[pltpu.VMEM((tm, d), BF16), pltpu.VMEM((tm, d), F32)],
        compiler_params=_params("parallel", "parallel"),
        name="ffn" + ("" if mode is None else "_" + mode) + ("_final" if final_norm else ""),
    )(*operands)


FOX_SLOT = 2 * FOX_HEAD_DIM
ONES_LANE = N_SPLIT * FOX_HEADS
FOX_VROWS = FOX_HEAD_DIM + 16
LOG2E = 1.4426950408889634


def _fox_proj_kernel(x_ref, shift_ref, scale_ref, ng_ref, wqk_ref, wvt_ref, vones_ref, wgt_ref,
                     wf_ref, bf_ref, tri_ref, place_ref, qa_ref, ka_ref, vt_ref, sgt_ref, carry_ref):
    @pl.when(pl.program_id(1) == 0)
    def _():
        carry_ref[...] = jnp.zeros_like(carry_ref)

    h = _modulated_norm(x_ref[0], ng_ref[...], scale_ref[0], shift_ref[0]).astype(BF16)
    z = _dot(h, wf_ref[...]) + bf_ref[...]
    logf = jnp.minimum(z, 0.0) - jnp.log(1.0 + jnp.exp(-jnp.abs(z)))
    pieces = jnp.concatenate(_split_bf16(logf, N_SPLIT), axis=1)
    part = _dot(tri_ref[...], pieces)
    cum = carry_ref[...] + ((part[:, :LANES] + part[:, LANES:2 * LANES]) + part[:, 2 * LANES:])
    carry_ref[...] = cum[-1:, :]

    lane = lax.broadcasted_iota(jnp.int32, cum.shape, 1)
    operand = jnp.where(lane == ONES_LANE, 1.0, 0.0).astype(F32)
    for j, p in enumerate(_split_bf16(cum * LOG2E, N_SPLIT)):
        pf = p.astype(F32)
        shifted = pf if j == 0 else pltpu.roll(pf, j * FOX_HEADS, axis=1)
        operand = jnp.where((lane >= j * FOX_HEADS) & (lane < (j + 1) * FOX_HEADS), shifted, operand)
    operand = operand.astype(BF16)
    d = wqk_ref.shape[0]
    slot_lane = lax.broadcasted_iota(jnp.int32, (operand.shape[0], FOX_SLOT), 1)
    width = FOX_HEADS * FOX_SLOT

    def project(w_cols, place_cols):
        packed = _dot(h, wqk_ref[:, w_cols:w_cols + d])
        return packed, _dot(operand, place_ref[:, place_cols:place_cols + width])

    def store_slots(out_ref, packed, decay):
        for hd in range(FOX_HEADS):
            pair = packed[:, (hd // 2) * FOX_SLOT:(hd // 2 + 1) * FOX_SLOT]
            if hd % 2:
                pair = pltpu.roll(pair, FOX_HEAD_DIM, axis=1)
            cols = slice(hd * FOX_SLOT, (hd + 1) * FOX_SLOT)
            out_ref[0, :, cols] = jnp.where(slot_lane < FOX_HEAD_DIM, pair, decay[:, cols]).astype(BF16)

    q_parts = project(0, 0)
    k_parts = project(d, width)
    store_slots(qa_ref, *q_parts)
    vt = _dot_nt(wvt_ref[...], h)
    store_slots(ka_ref, *k_parts)
    zg = _dot_nt(wgt_ref[...], h)
    vt_ref[0, 0] = (vt + vones_ref[...]).astype(BF16)
    sgt_ref[0, 0] = _sigmoid(zg).astype(BF16)


def _fox_placement():
    place = np.zeros((LANES, 2 * FOX_HEADS * FOX_SLOT), np.float32)
    k_off = FOX_HEADS * FOX_SLOT
    for h in range(FOX_HEADS):
        base = h * FOX_SLOT + FOX_HEAD_DIM
        for j in range(N_SPLIT):
            place[j * FOX_HEADS + h, base + j] = 1.0
            place[ONES_LANE, base + N_SPLIT + j] = 1.0
            place[ONES_LANE, k_off + base + j] = 1.0
            place[j * FOX_HEADS + h, k_off + base + N_SPLIT + j] = -1.0
    return jnp.asarray(place, BF16)


def _fox_proj(x, shift, scale, norm_g, w_in, b_f):
    b, s, d = x.shape
    hds, dh = FOX_HEADS, FOX_HEAD_DIM
    tm = ATT_BLOCK
    att_scale = dh ** -0.5

    wqk = jnp.concatenate([w_in[:, 0:d] * (att_scale * LOG2E), w_in[:, d:2 * d]], axis=1).astype(BF16)
    wv = w_in[:, 2 * d:3 * d].T.reshape(hds, dh, d)
    pad = FOX_VROWS - dh
    wvt = jnp.concatenate([wv, jnp.zeros((hds, pad, d), F32)], axis=1).reshape(hds * FOX_VROWS, d).astype(BF16)
    vones = jnp.concatenate([jnp.zeros((hds, dh, 1), F32), jnp.ones((hds, pad, 1), F32)],
                            axis=1).reshape(hds * FOX_VROWS, 1)
    wgt = w_in[:, 3 * d:4 * d].T.astype(BF16)
    wf = jnp.zeros((d, LANES), F32).at[:, :hds].set(w_in[:, 4 * d:]).astype(BF16)
    bf = jnp.zeros((1, LANES), F32).at[0, :hds].set(b_f)
    tri = jnp.tril(jnp.ones((tm, tm), F32)).astype(BF16)
    qw = hds * FOX_SLOT
    vrows = hds * FOX_VROWS
    return pl.pallas_call(
        _fox_proj_kernel,
        grid=(b, s // tm),
        in_specs=[
            _row_spec(tm, d), _batch_vec_spec(d), _batch_vec_spec(d), _resident((1, d)),
            _resident((d, 2 * d)), _resident((vrows, d)), _resident((vrows, 1)), _resident((d, d)),
            _resident((d, LANES)), _resident((1, LANES)), _resident((tm, tm)), _resident((LANES, 2 * qw)),
        ],
        out_specs=[_row_spec(tm, qw), _row_spec(tm, qw), _block_major_spec(vrows, tm),
                   _block_major_spec(d, tm)],
        out_shape=[
            jax.ShapeDtypeStruct((b, s, qw), BF16), jax.ShapeDtypeStruct((b, s, qw), BF16),
            jax.ShapeDtypeStruct((b, s // tm, vrows, tm), BF16), jax.ShapeDtypeStruct((b, s // tm, d, tm), BF16),
        ],
        scratch_shapes=[pltpu.VMEM((1, LANES), F32)],
        compiler_params=_params("parallel", "arbitrary"),
        name="fox_proj",
    )(x, shift, scale, norm_g.reshape(1, d), wqk, wvt, vones, wgt, wf, bf, tri, _fox_placement())


def _fox_attn_kernel(qa_ref, ka_ref, vt_ref, sgt_ref, o_ref,
                     acc_ref, m_ref, bias_ref, shift_ref, alpha_ref, s_ref, *, blk, nblk):
    dh = FOX_HEAD_DIM
    heads = range(2)
    half = blk // 2

    kp = lax.broadcasted_iota(jnp.int32, (blk, blk), 0)
    qp = lax.broadcasted_iota(jnp.int32, (blk, blk), 1)
    bias_ref[...] = jnp.where(kp <= qp, 0.0, -jnp.inf)
    acc_ref[...] = jnp.zeros_like(acc_ref)
    m_ref[...] = jnp.full_like(m_ref, -jnp.inf)

    def scores(ij, buf, diagonal, which):
        i, j = ij
        q = qa_ref[0, pl.ds(pl.multiple_of(i * blk, blk), blk), :]
        k = ka_ref[0, pl.ds(pl.multiple_of(j * blk, blk), blk), :]
        for hh in which:
            slot = slice(hh * FOX_SLOT, (hh + 1) * FOX_SLOT)
            m_old = m_ref[i, hh]
            if diagonal:
                s_lo = _dot_nt(k[:half, slot], q[:, slot]) + bias_ref[:half, :]
                s_hi = _dot_nt(k[half:, slot], q[half:, slot]) + bias_ref[half:, half:]
                top_lo = jnp.max(s_lo, axis=0, keepdims=True)
                top_hi = jnp.maximum(top_lo[:, half:], jnp.max(s_hi, axis=0, keepdims=True))
                m_new = jnp.maximum(m_old, jnp.concatenate([top_lo[:, :half], top_hi], axis=1))
                s_ref[buf, hh, :half, :] = s_lo
                s_ref[buf, hh, half:, half:] = s_hi
            else:
                s = _dot_nt(k[:, slot], q[:, slot])
                m_new = jnp.maximum(m_old, jnp.max(s, axis=0, keepdims=True))
                s_ref[buf, hh] = s
            m_ref[i, hh] = m_new
            shift_ref[buf, hh] = m_new
            alpha_ref[buf, hh] = jnp.exp2(m_old - m_new)

    def values(ij, buf, diagonal, which):
        i, j = ij
        vt = vt_ref[0, j]
        for hh in which:
            vth = vt[hh * FOX_VROWS:(hh + 1) * FOX_VROWS, :]
            shift = shift_ref[buf, hh]
            if diagonal:
                p_lo = jnp.exp2(s_ref[buf, hh, :half, :] - shift)
                p_hi = jnp.exp2(s_ref[buf, hh, half:, half:] - shift[:, half:])
                pv_lo = _dot(vth[:, :half], p_lo.astype(BF16))
                pv_hi = _dot(vth[:, half:], p_hi.astype(BF16))
                pv = jnp.concatenate([pv_lo[:, :half], pv_lo[:, half:] + pv_hi], axis=1)
                acc = alpha_ref[buf, hh] * acc_ref[i, hh] + pv
                rows = slice(hh * dh, (hh + 1) * dh)
                gated = acc[:dh] * (1.0 / acc[dh:dh + 1]) * sgt_ref[0, i, rows, :].astype(F32)
                o_ref[0, i, rows, :] = gated.astype(BF16)
            else:
                p = jnp.exp2(s_ref[buf, hh] - shift)
                acc_ref[i, hh] = alpha_ref[buf, hh] * acc_ref[i, hh] + _dot(vth, p.astype(BF16))

    def sweep(first, advance, n_pairs, diagonal, unroll):
        assert unroll % 2 == 0 and n_pairs > unroll

        def steps(carry, count):
            sc, va = carry
            for r in range(count):
                for hh in heads:
                    scores(sc, (r + 1) % 2, diagonal, (hh,))
                    values(va, r % 2, diagonal, (hh,))
                sc, va = advance(sc), advance(va)
            return sc, va

        scores(first, 0, diagonal, heads)
        carry = lax.fori_loop(0, (n_pairs - 1) // unroll, lambda t, c: steps(c, unroll),
                              (advance(first), first))
        tail = (n_pairs - 1) % unroll
        _, va = steps(carry, tail)
        values(va, tail % 2, diagonal, heads)

    def next_below_diagonal(ij):
        i, j = ij
        wrap = j + 1 == i
        return jnp.where(wrap, i + 1, i), jnp.where(wrap, 0, j + 1)

    zero, one = jnp.int32(0), jnp.int32(1)
    sweep((one, zero), next_below_diagonal, nblk * (nblk - 1) // 2, False, ATT_UNROLL)
    sweep((zero, zero), lambda ij: (ij[0] + 1, ij[1] + 1), nblk, True, ATT_UNROLL)


def _fox_attention(qa, ka, vt, sgt):
    b, nblk, d, blk = sgt.shape
    s = nblk * blk
    pairs = FOX_HEADS // 2
    pair_rows = 2 * FOX_HEAD_DIM
    kern = functools.partial(_fox_attn_kernel, blk=blk, nblk=nblk)
    return pl.pallas_call(
        kern,
        grid=(b, pairs),
        in_specs=[
            pl.BlockSpec((1, s, 2 * FOX_SLOT), lambda bb, p: (bb, 0, p)),
            pl.BlockSpec((1, s, 2 * FOX_SLOT), lambda bb, p: (bb, 0, p)),
            pl.BlockSpec((1, nblk, 2 * FOX_VROWS, blk), lambda bb, p: (bb, 0, p, 0)),
            pl.BlockSpec((1, nblk, pair_rows, blk), lambda bb, p: (bb, 0, p, 0)),
        ],
        out_specs=pl.BlockSpec((1, nblk, pair_rows, blk), lambda bb, p: (bb, 0, p, 0)),
        out_shape=jax.ShapeDtypeStruct((b, nblk, d, blk), BF16),
        scratch_shapes=[
            pltpu.VMEM((nblk, 2, FOX_VROWS, blk), F32),
            pltpu.VMEM((nblk, 2, 1, blk), F32),
            pltpu.VMEM((blk, blk), F32),
            pltpu.VMEM((2, 2, 1, blk), F32),
            pltpu.VMEM((2, 2, 1, blk), F32),
            pltpu.VMEM((2, 2, blk, blk), F32),
        ],
        compiler_params=_params("parallel", "parallel"),
        name="fox_attention",
    )(qa, ka, vt, sgt)


def _chunk_prefix_rows(x, chunk):
    n, lanes = x.shape
    row = lax.broadcasted_iota(jnp.int32, x.shape, 0) & (chunk - 1)
    shift = 1
    while shift < min(chunk, SUBLANES):
        x = x + jnp.where(row >= shift, pltpu.roll(x, shift, axis=0), 0.0)
        shift *= 2
    x = x.reshape(n // chunk, chunk, lanes)
    while shift < chunk:
        moved = jnp.concatenate([jnp.zeros((n // chunk, shift, lanes), F32), x[:, :chunk - shift]], axis=1)
        x = x + moved
        shift *= 2
    return x.reshape(n, lanes)


def _hgrn_proj_kernel(x_ref, shift_ref, scale_ref, ng_ref, w_ref, lbl_ref,
                      q_ref, kk_ref, g_ref, v_ref, sg_ref, *, d, layer, chunk):
    lg = lbl_ref[...]
    e = jnp.exp(lg - jnp.max(lg, axis=0, keepdims=True))
    sm = e / jnp.sum(e, axis=0, keepdims=True)
    lb = jnp.sum(sm[0:layer + 1], axis=0, keepdims=True) - sm[0:1]

    for r0 in range(0, x_ref.shape[1], HGRN_PROJ_SUBTILE):
        rows = slice(r0, r0 + HGRN_PROJ_SUBTILE)
        h = _modulated_norm(x_ref[0, rows, :], ng_ref[...], scale_ref[0], shift_ref[0]).astype(BF16)

        def project(task):
            section, c0 = task
            return _dot(h, w_ref[:, section * d + c0:section * d + c0 + HGRN_PROJ_COLS])

        def finish(task, z):
            section, c0 = task
            cols = slice(c0, c0 + HGRN_PROJ_COLS)
            if section == 0:
                q_ref[0, rows, cols] = z.astype(BF16)
            elif section == 1:
                f = lb[:, cols] + (1.0 - lb[:, cols]) * _sigmoid(z)
                kk_ref[0, rows, cols] = (1.0 - f).astype(BF16)
                g_ref[0, rows, cols] = _chunk_prefix_rows(jnp.log(f), chunk) * LOG2E
            elif section == 2:
                v_ref[0, rows, cols] = _silu(z).astype(BF16)
            else:
                sg_ref[0, rows, cols] = _silu(z).astype(BF16)

        tasks = [(section, c0) for section in range(4) for c0 in range(0, d, HGRN_PROJ_COLS)]
        z = project(tasks[0])
        for t, task in enumerate(tasks):
            nxt = project(tasks[t + 1]) if t + 1 < len(tasks) else None
            finish(task, z)
            z = nxt


def _hgrn_proj(x, shift, scale, norm_g, w_in, lb_logits, layer):
    b, s, d = x.shape
    depth = lb_logits.shape[0]
    tm = ROW_TILE // 2
    assert tm % HGRN_PROJ_SUBTILE == 0 and HGRN_PROJ_SUBTILE % HGRN_CHUNK == 0
    kern = functools.partial(_hgrn_proj_kernel, d=d, layer=layer, chunk=HGRN_CHUNK)
    bf = jax.ShapeDtypeStruct((b, s, d), BF16)
    return pl.pallas_call(
        kern,
        grid=(b, s // tm),
        in_specs=[
            _row_spec(tm, d), _batch_vec_spec(d), _batch_vec_spec(d), _resident((1, d)),
            _resident((d, 4 * d)), _resident((depth, d)),
        ],
        out_specs=[_row_spec(tm, d)] * 5,
        out_shape=[bf, bf, jax.ShapeDtypeStruct((b, s, d), F32), bf, bf],
        compiler_params=_params("parallel", "parallel"),
        name="hgrn_proj",
    )(x, shift, scale, norm_g.reshape(1, d), w_in.astype(BF16), lb_logits.astype(F32))


def _pivot_rows(g, half):
    n, lanes = g.shape
    if 2 * half >= SUBLANES:
        g3 = g.reshape(n // (2 * half), 2 * half, lanes)
        piv = jnp.broadcast_to(g3[:, half - 1:half, :], g3.shape)
        return piv.reshape(n, lanes)
    g3 = g.reshape(n // SUBLANES, SUBLANES, lanes)
    sub = lax.broadcasted_iota(jnp.int32, g3.shape, 1)
    out = None
    for grp in range(SUBLANES // (2 * half)):
        r = grp * 2 * half + half - 1
        cand = jnp.broadcast_to(g3[:, r:r + 1, :], g3.shape)
        out = cand if out is None else jnp.where(sub >= grp * 2 * half, cand, out)
    return out.reshape(n, lanes)


def _interleave_rows(lower, upper, half):
    n, lanes = lower.shape
    if half >= SUBLANES:
        lo3 = lower.reshape(n // (2 * half), 2 * half, lanes)
        up3 = upper.reshape(n // (2 * half), 2 * half, lanes)
        return jnp.concatenate([lo3[:, :half], up3[:, half:]], axis=1).reshape(n, lanes)
    row = lax.broadcasted_iota(jnp.int32, lower.shape, 0)
    return jnp.where((row & half) != 0, upper, lower)


def _hgrn_chunk_kernel(q_ref, kk_ref, g_ref, v_ref, sg_ref, gn_ref, o_ref, state_ref, level_ref,
                       *, chunk, heads):
    n_levels = chunk.bit_length() - 1

    @pl.when(pl.program_id(2) == 0)
    def _():
        state_ref[...] = jnp.zeros_like(state_ref)
        t_idx = lax.broadcasted_iota(jnp.int32, (chunk, chunk), 0)
        s_idx = lax.broadcasted_iota(jnp.int32, (chunk, chunk), 1)
        diff = t_idx ^ s_idx
        level = jnp.full((chunk, chunk), -1, jnp.int32)
        for lv in range(n_levels):
            level = level + (diff >= (1 << lv)).astype(jnp.int32)
        level_ref[...] = jnp.where(t_idx > s_idx, level, jnp.where(t_idx == s_idx, n_levels, -1))

    kdim = HGRN_EXPAND
    level = level_ref[...]
    at_level = [level == lv for lv in range(n_levels + 1)]
    row = lax.broadcasted_iota(jnp.int32, (chunk, kdim), 0)
    side = {1 << lv: jnp.where((row & (1 << lv)) != 0, 1.0, -1.0)
            for lv in range(n_levels) if (1 << lv) < SUBLANES}

    def level_exponent(g, half):
        if half in side:
            return (g - _pivot_rows(g, half)) * side[half]
        g3 = g.reshape(chunk // (2 * half), 2 * half, kdim)
        piv = g3[:, half - 1:half, :]
        return jnp.concatenate([piv - g3[:, :half], g3[:, half:] - piv], axis=1).reshape(chunk, kdim)

    def within_chunk(hh):
        cols = slice(hh * kdim, (hh + 1) * kdim)
        qb = q_ref[0, :, cols]
        kb = kk_ref[0, :, cols]
        q = qb.astype(F32)
        kk = kb.astype(F32)
        g = g_ref[0, :, cols]
        a = jnp.where(at_level[n_levels], _dot_nt(qb, kb), 0.0)
        for lv in range(n_levels):
            half = 1 << lv
            y = (_interleave_rows(kk, q, half) * jnp.exp2(level_exponent(g, half))).astype(BF16)
            a = jnp.where(at_level[lv], _dot_nt(y, y), a)
        return a.astype(BF16)

    def across_chunks(hh, a):
        cols = slice(hh * kdim, (hh + 1) * kdim)
        q = q_ref[0, :, cols].astype(F32)
        kk = kk_ref[0, :, cols].astype(F32)
        v = v_ref[0, :, cols]
        g = g_ref[0, :, cols]
        g_last = g[chunk - 1:chunk, :]
        state_t = state_ref[hh]
        o = _dot_nt((q * jnp.exp2(g)).astype(BF16), state_t.astype(BF16)) + _dot(a, v)
        k_tail = (kk * jnp.exp2(g_last - g)).astype(BF16)
        state_ref[hh] = state_t * jnp.exp2(g_last) + _dot_tn(v, k_tail)

        o = o * lax.rsqrt(jnp.mean(o * o, axis=-1, keepdims=True) + EPS)
        o = (o * gn_ref[:, cols]) * sg_ref[0, :, cols].astype(F32)
        o_ref[0, :, cols] = o.astype(BF16)

    pending = None
    for hh in range(heads):
        a = within_chunk(hh)
        if pending is not None:
            across_chunks(*pending)
        pending = (hh, a)
    across_chunks(*pending)


def _hgrn_chunks(q, kk, g, v, sg, g_norm):
    b, s, d = q.shape
    chunk, heads = HGRN_CHUNK, HGRN_HEADS_PER_STEP
    width = heads * HGRN_EXPAND
    spec = pl.BlockSpec((1, chunk, width), lambda bb, hp, c: (bb, c, hp))
    kern = functools.partial(_hgrn_chunk_kernel, chunk=chunk, heads=heads)
    return pl.pallas_call(
        kern,
        grid=(b, d // width, s // chunk),
        in_specs=[spec] * 5 + [pl.BlockSpec((1, width), lambda bb, hp, c: (0, hp))],
        out_specs=spec,
        out_shape=jax.ShapeDtypeStruct((b, s, d), BF16),
        scratch_shapes=[pltpu.VMEM((heads, HGRN_EXPAND, HGRN_EXPAND), F32),
                        pltpu.VMEM((chunk, chunk), jnp.int32)],
        compiler_params=_params("parallel", "parallel", "arbitrary"),
        name="hgrn_chunks",
    )(q, kk, g, v, sg, g_norm.reshape(1, d).astype(F32))


def kernel(x, c, ada_w, ada_b, norm_g, ffn_w_up, ffn_w_down, fox_w_in, fox_b_f, fox_w_out,
           hgrn_w_in, hgrn_norm_g, hgrn_w_out, hgrn_lb_logits, final_norm_g):
    b, s, d = x.shape
    depth = ada_w.shape[0]
    mod = _ada_modulation(c, ada_w, ada_b).reshape(depth, b, N_SUB, 3, 1, d)
    w_up, w_down = ffn_w_up.astype(BF16), ffn_w_down.astype(BF16)
    for i in range(depth):
        shift = lambda sub: mod[i, :, sub, 0]
        scale = lambda sub: mod[i, :, sub, 1]
        gate = lambda sub: mod[i, :, sub, 2]
        x = _ffn(x, shift(0), scale(0), gate(0), norm_g[i, 0], w_up, w_down, (i, 0),
                 final_norm_g, False)
        j = i // 2
        if i % 2 == 0:
            qa, ka, vt, sgt = _fox_proj(x, shift(1), scale(1), norm_g[i, 1], fox_w_in[j], fox_b_f[j])
            mixer = (_fox_attention(qa, ka, vt, sgt), gate(1), fox_w_out[j])
        else:
            q, kk, g, v, sg = _hgrn_proj(x, shift(1), scale(1), norm_g[i, 1], hgrn_w_in[j],
                                         hgrn_lb_logits, i)
            mixer = (_hgrn_chunks(q, kk, g, v, sg, hgrn_norm_g[j]), gate(1), hgrn_w_out[j])
        x = _ffn(x, shift(2), scale(2), gate(2), norm_g[i, 2], w_up, w_down, (i, 1),
                 final_norm_g, i == depth - 1, mixer)
    return x
```

```python
import functools

import jax
import jax.numpy as jnp
import numpy as np
from jax import lax
from jax.experimental import pallas as pl
from jax.experimental.pallas import tpu as pltpu

N_SUB = 3
FOX_HEADS = 16
FOX_HEAD_DIM = 64
HGRN_EXPAND = 128
EPS = 1e-6

LANES = 128
SUBLANES = 8
VMEM_LIMIT_BYTES = 56 * 1024 * 1024

ROW_TILE = 1024
FFN_CHUNK = 256
ATT_BLOCK = 512
ATT_UNROLL = 8
HGRN_CHUNK = 128
HGRN_HEADS_PER_STEP = 8
HGRN_CHUNKS_PER_STEP = 2
HGRN_PROJ_SUBTILE = 256
HGRN_PROJ_COLS = 256
N_SPLIT = 3

BF16 = jnp.bfloat16
F32 = jnp.float32


def _params(*sem):
    return pltpu.CompilerParams(dimension_semantics=sem, vmem_limit_bytes=VMEM_LIMIT_BYTES)


def _resident(shape):
    zeros = (0,) * len(shape)
    return pl.BlockSpec(shape, lambda *_: zeros, pipeline_mode=pl.Buffered(1))


def _row_spec(tm, width):
    return pl.BlockSpec((1, tm, width), lambda b, i: (b, i, 0))


def _block_major_spec(features, tm):
    return pl.BlockSpec((1, 1, features, tm), lambda b, i: (b, i, 0, 0))


def _batch_vec_spec(width):
    return pl.BlockSpec((1, 1, width), lambda b, i: (b, 0, 0))


def _dot(a, b):
    return jnp.dot(a, b, preferred_element_type=F32)


def _dot_nt(a, b):
    return lax.dot_general(a, b, (((1,), (1,)), ((), ())), preferred_element_type=F32)


def _dot_tn(a, b):
    return lax.dot_general(a, b, (((0,), (0,)), ((), ())), preferred_element_type=F32)


def _sigmoid(x):
    return 1.0 / (1.0 + jnp.exp(-x))


def _silu(x):
    return x * _sigmoid(x)


def _modulated_norm(x, norm_g, scale, shift):
    y = x * lax.rsqrt(jnp.mean(x * x, axis=-1, keepdims=True) + EPS)
    return (y * norm_g) * (1.0 + scale) + shift


def _split_bf16(x, n):
    pieces = []
    rest = x
    for _ in range(n):
        p = rest.astype(BF16)
        pieces.append(p)
        rest = rest - p.astype(F32)
    return pieces


def _ada_kernel(c_ref, w_ref, b_ref, o_ref):
    cond = _silu(c_ref[...])
    y = jnp.dot(cond, w_ref[0], preferred_element_type=F32,
                precision=lax.Precision.HIGHEST)
    o_ref[0] = y + b_ref[0]


def _ada_modulation(c, ada_w, ada_b):
    depth, d, n = ada_w.shape
    b = c.shape[0]
    bp = -(-b // SUBLANES) * SUBLANES
    tn = n // 9 if n % (9 * LANES) == 0 else n
    c_pad = jnp.zeros((bp, d), F32).at[:b].set(c)
    out = pl.pallas_call(
        _ada_kernel,
        grid=(depth, n // tn),
        in_specs=[
            pl.BlockSpec((bp, d), lambda i, j: (0, 0)),
            pl.BlockSpec((1, d, tn), lambda i, j: (i, 0, j)),
            pl.BlockSpec((1, 1, tn), lambda i, j: (i, 0, j)),
        ],
        out_specs=pl.BlockSpec((1, bp, tn), lambda i, j: (i, 0, j)),
        out_shape=jax.ShapeDtypeStruct((depth, bp, n), F32),
        compiler_params=_params("parallel", "parallel"),
        name="ada_modulation",
    )(c_pad, ada_w, ada_b.reshape(depth, 1, n))
    return out[:, :b]


def _ffn_kernel(*refs, tf, mixer, final_norm):
    if mixer is None:
        x_ref, shift_ref, scale_ref, gate_ref, ng_ref, wup_ref, wdn_ref, fg_ref = refs[:8]
    else:
        (x_ref, mix_ref, mgate_ref, wout_ref,
         shift_ref, scale_ref, gate_ref, ng_ref, wup_ref, wdn_ref, fg_ref) = refs[:11]
    o_ref, h_ref, acc_ref = refs[-3:]
    wup_ref, wdn_ref = wup_ref.at[0, 0], wdn_ref.at[0, 0]
    hidden = wdn_ref.shape[0]

    x = x_ref[0]
    if mixer == "rows":
        x = x + mgate_ref[0] * _dot(mix_ref[0], wout_ref[...])
    elif mixer == "features":
        parts = [_dot_tn(mix_ref[0, blk], wout_ref[...]) for blk in range(mix_ref.shape[1])]
        x = x + mgate_ref[0] * jnp.concatenate(parts, axis=0)
    o_ref[0] = x
    h_ref[...] = _modulated_norm(x, ng_ref[...], scale_ref[0], shift_ref[0]).astype(BF16)

    def up(f):
        return (_dot(h_ref[...], wup_ref[:, f:f + tf]),
                _dot(h_ref[...], wup_ref[:, hidden + f:hidden + f + tf]))

    ab = up(0)
    for f in range(0, hidden, tf):
        nxt = up(f + tf) if f + tf < hidden else None
        down = _dot((_silu(ab[0]) * ab[1]).astype(BF16), wdn_ref[f:f + tf, :])
        if f == 0:
            acc_ref[...] = down
        else:
            acc_ref[...] += down
        ab = nxt

    y = o_ref[0] + (0.5 * gate_ref[0]) * acc_ref[...]
    if final_norm:
        y = (y * lax.rsqrt(jnp.mean(y * y, axis=-1, keepdims=True) + EPS)) * fg_ref[...]
    o_ref[0] = y


def _ffn(x, shift, scale, gate, norm_g, w_up, w_down, which, final_g, final_norm, mixer=None):
    b, s, d = x.shape
    f = w_down.shape[2]
    stack_index = tuple(which) + (0, 0)
    stack_spec = lambda rows, cols: pl.BlockSpec((1, 1, rows, cols), lambda bb, i: stack_index,
                                                 pipeline_mode=pl.Buffered(1))
    tf = FFN_CHUNK
    assert f % tf == 0
    tm = ROW_TILE
    operands, specs, mode = [x], [_row_spec(tm, d)], None
    if mixer is not None:
        o, mgate, w_out = mixer
        if o.ndim == 4:
            mode, blk = "features", o.shape[3]
            o_spec = pl.BlockSpec((1, tm // blk, d, blk), lambda bb, i: (bb, i, 0, 0))
        else:
            mode, o_spec = "rows", _row_spec(tm, d)
        operands += [o, mgate, w_out.astype(BF16)]
        specs += [o_spec, _batch_vec_spec(d), _resident((d, d))]
    operands += [shift, scale, gate, norm_g.reshape(1, d), w_up, w_down, final_g.reshape(1, d)]
    specs += [_batch_vec_spec(d), _batch_vec_spec(d), _batch_vec_spec(d), _resident((1, d)),
              stack_spec(d, 2 * f), stack_spec(f, d), _resident((1, d))]
    kern = functools.partial(_ffn_kernel, tf=tf, mixer=mode, final_norm=final_norm)
    return pl.pallas_call(
        kern,
        grid=(b, s // tm),
        in_specs=specs,
        out_specs=_row_spec(tm, d),
        out_shape=jax.ShapeDtypeStruct((b, s, d), F32),
        scratch_shapes=[pltpu.VMEM((tm, d), BF16), pltpu.VMEM((tm, d), F32)],
        compiler_params=_params("parallel", "parallel"),
        name="ffn" + ("" if mode is None else "_" + mode) + ("_final" if final_norm else ""),
    )(*operands)


FOX_SLOT = 2 * FOX_HEAD_DIM
ONES_LANE = N_SPLIT * FOX_HEADS
FOX_VROWS = FOX_HEAD_DIM + 16
LOG2E = 1.4426950408889634


def _fox_proj_kernel(x_ref, shift_ref, scale_ref, ng_ref, wqk_ref, wvt_ref, vones_ref, wgt_ref,
                     wf_ref, bf_ref, tri_ref, place_ref, qa_ref, ka_ref, vt_ref, sgt_ref, carry_ref):
    @pl.when(pl.program_id(1) == 0)
    def _():
        carry_ref[...] = jnp.zeros_like(carry_ref)

    h = _modulated_norm(x_ref[0], ng_ref[...], scale_ref[0], shift_ref[0]).astype(BF16)
    z = _dot(h, wf_ref[...]) + bf_ref[...]
    logf = jnp.minimum(z, 0.0) - jnp.log(1.0 + jnp.exp(-jnp.abs(z)))
    pieces = jnp.concatenate(_split_bf16(logf, N_SPLIT), axis=1)
    part = _dot(tri_ref[...], pieces)
    cum = carry_ref[...] + ((part[:, :LANES] + part[:, LANES:2 * LANES]) + part[:, 2 * LANES:])
    carry_ref[...] = cum[-1:, :]

    lane = lax.broadcasted_iota(jnp.int32, cum.shape, 1)
    operand = jnp.where(lane == ONES_LANE, 1.0, 0.0).astype(F32)
    for j, p in enumerate(_split_bf16(cum * LOG2E, N_SPLIT)):
        pf = p.astype(F32)
        shifted = pf if j == 0 else pltpu.roll(pf, j * FOX_HEADS, axis=1)
        operand = jnp.where((lane >= j * FOX_HEADS) & (lane < (j + 1) * FOX_HEADS), shifted, operand)
    operand = operand.astype(BF16)
    d = wqk_ref.shape[0]
    slot_lane = lax.broadcasted_iota(jnp.int32, (operand.shape[0], FOX_SLOT), 1)
    width = FOX_HEADS * FOX_SLOT

    def project(w_cols, place_cols):
        packed = _dot(h, wqk_ref[:, w_cols:w_cols + d])
        return packed, _dot(operand, place_ref[:, place_cols:place_cols + width])

    def store_slots(out_ref, packed, decay):
        for hd in range(FOX_HEADS):
            pair = packed[:, (hd // 2) * FOX_SLOT:(hd // 2 + 1) * FOX_SLOT]
            if hd % 2:
                pair = pltpu.roll(pair, FOX_HEAD_DIM, axis=1)
            cols = slice(hd * FOX_SLOT, (hd + 1) * FOX_SLOT)
            out_ref[0, :, cols] = jnp.where(slot_lane < FOX_HEAD_DIM, pair, decay[:, cols]).astype(BF16)

    q_parts = project(0, 0)
    k_parts = project(d, width)
    store_slots(qa_ref, *q_parts)
    vt = _dot_nt(wvt_ref[...], h)
    store_slots(ka_ref, *k_parts)
    zg = _dot_nt(wgt_ref[...], h)
    vt_ref[0, 0] = (vt + vones_ref[...]).astype(BF16)
    sgt_ref[0, 0] = _sigmoid(zg).astype(BF16)


def _fox_placement():
    place = np.zeros((LANES, 2 * FOX_HEADS * FOX_SLOT), np.float32)
    k_off = FOX_HEADS * FOX_SLOT
    for h in range(FOX_HEADS):
        base = h * FOX_SLOT + FOX_HEAD_DIM
        for j in range(N_SPLIT):
            place[j * FOX_HEADS + h, base + j] = 1.0
            place[ONES_LANE, base + N_SPLIT + j] = 1.0
            place[ONES_LANE, k_off + base + j] = 1.0
            place[j * FOX_HEADS + h, k_off + base + N_SPLIT + j] = -1.0
    return jnp.asarray(place, BF16)


def _fox_proj(x, shift, scale, norm_g, w_in, b_f):
    b, s, d = x.shape
    hds, dh = FOX_HEADS, FOX_HEAD_DIM
    tm = ATT_BLOCK
    att_scale = dh ** -0.5

    wqk = jnp.concatenate([w_in[:, 0:d] * (att_scale * LOG2E), w_in[:, d:2 * d]], axis=1).astype(BF16)
    wv = w_in[:, 2 * d:3 * d].T.reshape(hds, dh, d)
    pad = FOX_VROWS - dh
    wvt = jnp.concatenate([wv, jnp.zeros((hds, pad, d), F32)], axis=1).reshape(hds * FOX_VROWS, d).astype(BF16)
    vones = jnp.concatenate([jnp.zeros((hds, dh, 1), F32), jnp.ones((hds, pad, 1), F32)],
                            axis=1).reshape(hds * FOX_VROWS, 1)
    wgt = w_in[:, 3 * d:4 * d].T.astype(BF16)
    wf = jnp.zeros((d, LANES), F32).at[:, :hds].set(w_in[:, 4 * d:]).astype(BF16)
    bf = jnp.zeros((1, LANES), F32).at[0, :hds].set(b_f)
    tri = jnp.tril(jnp.ones((tm, tm), F32)).astype(BF16)
    qw = hds * FOX_SLOT
    vrows = hds * FOX_VROWS
    return pl.pallas_call(
        _fox_proj_kernel,
        grid=(b, s // tm),
        in_specs=[
            _row_spec(tm, d), _batch_vec_spec(d), _batch_vec_spec(d), _resident((1, d)),
            _resident((d, 2 * d)), _resident((vrows, d)), _resident((vrows, 1)), _resident((d, d)),
            _resident((d, LANES)), _resident((1, LANES)), _resident((tm, tm)), _resident((LANES, 2 * qw)),
        ],
        out_specs=[_row_spec(tm, qw), _row_spec(tm, qw), _block_major_spec(vrows, tm),
                   _block_major_spec(d, tm)],
        out_shape=[
            jax.ShapeDtypeStruct((b, s, qw), BF16), jax.ShapeDtypeStruct((b, s, qw), BF16),
            jax.ShapeDtypeStruct((b, s // tm, vrows, tm), BF16), jax.ShapeDtypeStruct((b, s // tm, d, tm), BF16),
        ],
        scratch_shapes=[pltpu.VMEM((1, LANES), F32)],
        compiler_params=_params("parallel", "arbitrary"),
        name="fox_proj",
    )(x, shift, scale, norm_g.reshape(1, d), wqk, wvt, vones, wgt, wf, bf, tri, _fox_placement())


def _fox_attn_kernel(qa_ref, ka_ref, vt_ref, sgt_ref, o_ref,
                     acc_ref, m_ref, bias_ref, shift_ref, alpha_ref, s_ref, *, blk, nblk):
    dh = FOX_HEAD_DIM
    heads = range(2)
    half = blk // 2

    kp = lax.broadcasted_iota(jnp.int32, (blk, blk), 0)
    qp = lax.broadcasted_iota(jnp.int32, (blk, blk), 1)
    bias_ref[...] = jnp.where(kp <= qp, 0.0, -jnp.inf)
    acc_ref[...] = jnp.zeros_like(acc_ref)
    m_ref[...] = jnp.full_like(m_ref, -jnp.inf)

    def scores(ij, buf, diagonal, which):
        i, j = ij
        q = qa_ref[0, pl.ds(pl.multiple_of(i * blk, blk), blk), :]
        k = ka_ref[0, pl.ds(pl.multiple_of(j * blk, blk), blk), :]
        for hh in which:
            slot = slice(hh * FOX_SLOT, (hh + 1) * FOX_SLOT)
            m_old = m_ref[i, hh]
            if diagonal:
                s_lo = _dot_nt(k[:half, slot], q[:, slot]) + bias_ref[:half, :]
                s_hi = _dot_nt(k[half:, slot], q[half:, slot]) + bias_ref[half:, half:]
                top_lo = jnp.max(s_lo, axis=0, keepdims=True)
                top_hi = jnp.maximum(top_lo[:, half:], jnp.max(s_hi, axis=0, keepdims=True))
                m_new = jnp.maximum(m_old, jnp.concatenate([top_lo[:, :half], top_hi], axis=1))
                s_ref[buf, hh, :half, :] = s_lo
                s_ref[buf, hh, half:, half:] = s_hi
            else:
                s = _dot_nt(k[:, slot], q[:, slot])
                m_new = jnp.maximum(m_old, jnp.max(s, axis=0, keepdims=True))
                s_ref[buf, hh] = s
            m_ref[i, hh] = m_new
            shift_ref[buf, hh] = m_new
            alpha_ref[buf, hh] = jnp.exp2(m_old - m_new)

    def values(ij, buf, diagonal, which):
        i, j = ij
        vt = vt_ref[0, j]
        for hh in which:
            vth = vt[hh * FOX_VROWS:(hh + 1) * FOX_VROWS, :]
            shift = shift_ref[buf, hh]
            if diagonal:
                p_lo = jnp.exp2(s_ref[buf, hh, :half, :] - shift)
                p_hi = jnp.exp2(s_ref[buf, hh, half:, half:] - shift[:, half:])
                pv_lo = _dot(vth[:, :half], p_lo.astype(BF16))
                pv_hi = _dot(vth[:, half:], p_hi.astype(BF16))
                pv = jnp.concatenate([pv_lo[:, :half], pv_lo[:, half:] + pv_hi], axis=1)
                acc = alpha_ref[buf, hh] * acc_ref[i, hh] + pv
                rows = slice(hh * dh, (hh + 1) * dh)
                gated = acc[:dh] * (1.0 / acc[dh:dh + 1]) * sgt_ref[0, i, rows, :].astype(F32)
                o_ref[0, i, rows, :] = gated.astype(BF16)
            else:
                p = jnp.exp2(s_ref[buf, hh] - shift)
                acc_ref[i, hh] = alpha_ref[buf, hh] * acc_ref[i, hh] + _dot(vth, p.astype(BF16))

    def sweep(first, advance, n_pairs, diagonal, unroll):
        assert unroll % 2 == 0 and n_pairs > unroll

        def steps(carry, count):
            sc, va = carry
            for r in range(count):
                for hh in heads:
                    scores(sc, (r + 1) % 2, diagonal, (hh,))
                    values(va, r % 2, diagonal, (hh,))
                sc, va = advance(sc), advance(va)
            return sc, va

        scores(first, 0, diagonal, heads)
        carry = lax.fori_loop(0, (n_pairs - 1) // unroll, lambda t, c: steps(c, unroll),
                              (advance(first), first))
        tail = (n_pairs - 1) % unroll
        _, va = steps(carry, tail)
        values(va, tail % 2, diagonal, heads)

    def next_below_diagonal(ij):
        i, j = ij
        wrap = j + 1 == i
        return jnp.where(wrap, i + 1, i), jnp.where(wrap, 0, j + 1)

    zero, one = jnp.int32(0), jnp.int32(1)
    sweep((one, zero), next_below_diagonal, nblk * (nblk - 1) // 2, False, ATT_UNROLL)
    sweep((zero, zero), lambda ij: (ij[0] + 1, ij[1] + 1), nblk, True, ATT_UNROLL)


def _fox_attention(qa, ka, vt, sgt):
    b, nblk, d, blk = sgt.shape
    s = nblk * blk
    pairs = FOX_HEADS // 2
    pair_rows = 2 * FOX_HEAD_DIM
    kern = functools.partial(_fox_attn_kernel, blk=blk, nblk=nblk)
    return pl.pallas_call(
        kern,
        grid=(b, pairs),
        in_specs=[
            pl.BlockSpec((1, s, 2 * FOX_SLOT), lambda bb, p: (bb, 0, p)),
            pl.BlockSpec((1, s, 2 * FOX_SLOT), lambda bb, p: (bb, 0, p)),
            pl.BlockSpec((1, nblk, 2 * FOX_VROWS, blk), lambda bb, p: (bb, 0, p, 0)),
            pl.BlockSpec((1, nblk, pair_rows, blk), lambda bb, p: (bb, 0, p, 0)),
        ],
        out_specs=pl.BlockSpec((1, nblk, pair_rows, blk), lambda bb, p: (bb, 0, p, 0)),
        out_shape=jax.ShapeDtypeStruct((b, nblk, d, blk), BF16),
        scratch_shapes=[
            pltpu.VMEM((nblk, 2, FOX_VROWS, blk), F32),
            pltpu.VMEM((nblk, 2, 1, blk), F32),
            pltpu.VMEM((blk, blk), F32),
            pltpu.VMEM((2, 2, 1, blk), F32),
            pltpu.VMEM((2, 2, 1, blk), F32),
            pltpu.VMEM((2, 2, blk, blk), F32),
        ],
        compiler_params=_params("parallel", "parallel"),
        name="fox_attention",
    )(qa, ka, vt, sgt)


def _chunk_prefix_rows(x, chunk):
    n, lanes = x.shape
    row = lax.broadcasted_iota(jnp.int32, x.shape, 0) & (chunk - 1)
    shift = 1
    while shift < min(chunk, SUBLANES):
        x = x + jnp.where(row >= shift, pltpu.roll(x, shift, axis=0), 0.0)
        shift *= 2
    x = x.reshape(n // chunk, chunk, lanes)
    while shift < chunk:
        moved = jnp.concatenate([jnp.zeros((n // chunk, shift, lanes), F32), x[:, :chunk - shift]], axis=1)
        x = x + moved
        shift *= 2
    return x.reshape(n, lanes)


def _hgrn_proj_kernel(x_ref, shift_ref, scale_ref, ng_ref, w_ref, lbl_ref,
                      q_ref, kk_ref, g_ref, v_ref, sg_ref, *, d, layer, chunk):
    lg = lbl_ref[...]
    e = jnp.exp(lg - jnp.max(lg, axis=0, keepdims=True))
    sm = e / jnp.sum(e, axis=0, keepdims=True)
    lb = jnp.sum(sm[0:layer + 1], axis=0, keepdims=True) - sm[0:1]

    for r0 in range(0, x_ref.shape[1], HGRN_PROJ_SUBTILE):
        rows = slice(r0, r0 + HGRN_PROJ_SUBTILE)
        h = _modulated_norm(x_ref[0, rows, :], ng_ref[...], scale_ref[0], shift_ref[0]).astype(BF16)

        def project(task):
            section, c0 = task
            return _dot(h, w_ref[:, section * d + c0:section * d + c0 + HGRN_PROJ_COLS])

        def finish(task, z):
            section, c0 = task
            cols = slice(c0, c0 + HGRN_PROJ_COLS)
            if section == 0:
                q_ref[0, rows, cols] = z.astype(BF16)
            elif section == 1:
                f = lb[:, cols] + (1.0 - lb[:, cols]) * _sigmoid(z)
                kk_ref[0, rows, cols] = (1.0 - f).astype(BF16)
                g_ref[0, rows, cols] = _chunk_prefix_rows(jnp.log(f), chunk) * LOG2E
            elif section == 2:
                v_ref[0, rows, cols] = _silu(z).astype(BF16)
            else:
                sg_ref[0, rows, cols] = _silu(z).astype(BF16)

        tasks = [(section, c0) for section in range(4) for c0 in range(0, d, HGRN_PROJ_COLS)]
        z = project(tasks[0])
        for t, task in enumerate(tasks):
            nxt = project(tasks[t + 1]) if t + 1 < len(tasks) else None
            finish(task, z)
            z = nxt


def _hgrn_proj(x, shift, scale, norm_g, w_in, lb_logits, layer):
    b, s, d = x.shape
    depth = lb_logits.shape[0]
    tm = ROW_TILE // 2
    assert tm % HGRN_PROJ_SUBTILE == 0 and HGRN_PROJ_SUBTILE % HGRN_CHUNK == 0
    kern = functools.partial(_hgrn_proj_kernel, d=d, layer=layer, chunk=HGRN_CHUNK)
    bf = jax.ShapeDtypeStruct((b, s, d), BF16)
    return pl.pallas_call(
        kern,
        grid=(b, s // tm),
        in_specs=[
            _row_spec(tm, d), _batch_vec_spec(d), _batch_vec_spec(d), _resident((1, d)),
            _resident((d, 4 * d)), _resident((depth, d)),
        ],
        out_specs=[_row_spec(tm, d)] * 5,
        out_shape=[bf, bf, jax.ShapeDtypeStruct((b, s, d), F32), bf, bf],
        compiler_params=_params("parallel", "parallel"),
        name="hgrn_proj",
    )(x, shift, scale, norm_g.reshape(1, d), w_in.astype(BF16), lb_logits.astype(F32))


def _pivot_rows(g, half):
    n, lanes = g.shape
    if 2 * half >= SUBLANES:
        g3 = g.reshape(n // (2 * half), 2 * half, lanes)
        piv = jnp.broadcast_to(g3[:, half - 1:half, :], g3.shape)
        return piv.reshape(n, lanes)
    g3 = g.reshape(n // SUBLANES, SUBLANES, lanes)
    sub = lax.broadcasted_iota(jnp.int32, g3.shape, 1)
    out = None
    for grp in range(SUBLANES // (2 * half)):
        r = grp * 2 * half + half - 1
        cand = jnp.broadcast_to(g3[:, r:r + 1, :], g3.shape)
        out = cand if out is None else jnp.where(sub >= grp * 2 * half, cand, out)
    return out.reshape(n, lanes)


def _interleave_rows(lower, upper, half):
    n, lanes = lower.shape
    if half >= SUBLANES:
        lo3 = lower.reshape(n // (2 * half), 2 * half, lanes)
        up3 = upper.reshape(n // (2 * half), 2 * half, lanes)
        return jnp.concatenate([lo3[:, :half], up3[:, half:]], axis=1).reshape(n, lanes)
    row = lax.broadcasted_iota(jnp.int32, lower.shape, 0)
    return jnp.where((row & half) != 0, upper, lower)


def _hgrn_chunk_kernel(q_ref, kk_ref, g_ref, v_ref, sg_ref, gn_ref, o_ref, state_ref, level_ref,
                       *, chunk, heads):
    n_levels = chunk.bit_length() - 1

    @pl.when(pl.program_id(2) == 0)
    def _():
        state_ref[...] = jnp.zeros_like(state_ref)
        t_idx = lax.broadcasted_iota(jnp.int32, (chunk, chunk), 0)
        s_idx = lax.broadcasted_iota(jnp.int32, (chunk, chunk), 1)
        diff = t_idx ^ s_idx
        level = jnp.full((chunk, chunk), -1, jnp.int32)
        for lv in range(n_levels):
            level = level + (diff >= (1 << lv)).astype(jnp.int32)
        level_ref[...] = jnp.where(t_idx > s_idx, level, jnp.where(t_idx == s_idx, n_levels, -1))

    kdim = HGRN_EXPAND
    level = level_ref[...]
    at_level = [level == lv for lv in range(n_levels + 1)]
    row = lax.broadcasted_iota(jnp.int32, (chunk, kdim), 0)
    side = {1 << lv: jnp.where((row & (1 << lv)) != 0, 1.0, -1.0)
            for lv in range(n_levels) if (1 << lv) < SUBLANES}

    def level_exponent(g, half):
        if half in side:
            return (g - _pivot_rows(g, half)) * side[half]
        g3 = g.reshape(chunk // (2 * half), 2 * half, kdim)
        piv = g3[:, half - 1:half, :]
        return jnp.concatenate([piv - g3[:, :half], g3[:, half:] - piv], axis=1).reshape(chunk, kdim)

    def within_chunk(ci, hh):
        rows = slice(ci * chunk, (ci + 1) * chunk)
        cols = slice(hh * kdim, (hh + 1) * kdim)
        qb = q_ref[0, rows, cols]
        kb = kk_ref[0, rows, cols]
        q = qb.astype(F32)
        kk = kb.astype(F32)
        g = g_ref[0, rows, cols]
        a = jnp.where(at_level[n_levels], _dot_nt(qb, kb), 0.0)
        for lv in range(n_levels):
            half = 1 << lv
            y = (_interleave_rows(kk, q, half) * jnp.exp2(level_exponent(g, half))).astype(BF16)
            a = jnp.where(at_level[lv], _dot_nt(y, y), a)
        return a.astype(BF16)

    def across_chunks(ci, hh, a):
        rows = slice(ci * chunk, (ci + 1) * chunk)
        cols = slice(hh * kdim, (hh + 1) * kdim)
        q = q_ref[0, rows, cols].astype(F32)
        kk = kk_ref[0, rows, cols].astype(F32)
        v = v_ref[0, rows, cols]
        g = g_ref[0, rows, cols]
        g_last = g[chunk - 1:chunk, :]
        state_t = state_ref[hh]
        o = _dot_nt((q * jnp.exp2(g)).astype(BF16), state_t.astype(BF16)) + _dot(a, v)
        k_tail = (kk * jnp.exp2(g_last - g)).astype(BF16)
        state_ref[hh] = state_t * jnp.exp2(g_last) + _dot_tn(v, k_tail)

        o = o * lax.rsqrt(jnp.mean(o * o, axis=-1, keepdims=True) + EPS)
        o = (o * gn_ref[:, cols]) * sg_ref[0, rows, cols].astype(F32)
        o_ref[0, rows, cols] = o.astype(BF16)

    pending = None
    for ci in range(q_ref.shape[1] // chunk):
        for hh in range(heads):
            a = within_chunk(ci, hh)
            if pending is not None:
                across_chunks(*pending)
            pending = (ci, hh, a)
    across_chunks(*pending)


def _hgrn_chunks(q, kk, g, v, sg, g_norm):
    b, s, d = q.shape
    chunk, heads = HGRN_CHUNK, HGRN_HEADS_PER_STEP
    width = heads * HGRN_EXPAND
    step_rows = HGRN_CHUNKS_PER_STEP * chunk
    spec = pl.BlockSpec((1, step_rows, width), lambda bb, hp, c: (bb, c, hp))
    kern = functools.partial(_hgrn_chunk_kernel, chunk=chunk, heads=heads)
    return pl.pallas_call(
        kern,
        grid=(b, d // width, s // step_rows),
        in_specs=[spec] * 5 + [pl.BlockSpec((1, width), lambda bb, hp, c: (0, hp))],
        out_specs=spec,
        out_shape=jax.ShapeDtypeStruct((b, s, d), BF16),
        scratch_shapes=[pltpu.VMEM((heads, HGRN_EXPAND, HGRN_EXPAND), F32),
                        pltpu.VMEM((chunk, chunk), jnp.int32)],
        compiler_params=_params("parallel", "parallel", "arbitrary"),
        name="hgrn_chunks",
    )(q, kk, g, v, sg, g_norm.reshape(1, d).astype(F32))


def kernel(x, c, ada_w, ada_b, norm_g, ffn_w_up, ffn_w_down, fox_w_in, fox_b_f, fox_w_out,
           hgrn_w_in, hgrn_norm_g, hgrn_w_out, hgrn_lb_logits, final_norm_g):
    b, s, d = x.shape
    depth = ada_w.shape[0]
    mod = _ada_modulation(c, ada_w, ada_b).reshape(depth, b, N_SUB, 3, 1, d)
    w_up, w_down = ffn_w_up.astype(BF16), ffn_w_down.astype(BF16)
    for i in range(depth):
        shift = lambda sub: mod[i, :, sub, 0]
        scale = lambda sub: mod[i, :, sub, 1]
        gate = lambda sub: mod[i, :, sub, 2]
        x = _ffn(x, shift(0), scale(0), gate(0), norm_g[i, 0], w_up, w_down, (i, 0),
                 final_norm_g, False)
        j = i // 2
        if i % 2 == 0:
            qa, ka, vt, sgt = _fox_proj(x, shift(1), scale(1), norm_g[i, 1], fox_w_in[j], fox_b_f[j])
            mixer = (_fox_attention(qa, ka, vt, sgt), gate(1), fox_w_out[j])
        else:
            q, kk, g, v, sg = _hgrn_proj(x, shift(1), scale(1), norm_g[i, 1], hgrn_w_in[j],
                                         hgrn_lb_logits, i)
            mixer = (_hgrn_chunks(q, kk, g, v, sg, hgrn_norm_g[j]), gate(1), hgrn_w_out[j])
        x = _ffn(x, shift(2), scale(2), gate(2), norm_g[i, 2], w_up, w_down, (i, 1),
                 final_norm_g, i == depth - 1, mixer)
    return x
```

```python
import functools

import jax
import jax.numpy as jnp
import numpy as np
from jax import lax
from jax.experimental import pallas as pl
from jax.experimental.pallas import tpu as pltpu

N_SUB = 3
FOX_HEADS = 16
FOX_HEAD_DIM = 64
HGRN_EXPAND = 128
EPS = 1e-6

LANES = 128
SUBLANES = 8
VMEM_LIMIT_BYTES = 56 * 1024 * 1024

ROW_TILE = 1024
FFN_CHUNK = 256
ATT_BLOCK = 512
ATT_UNROLL = 8
HGRN_CHUNK = 128
HGRN_HEADS_PER_STEP = 8
HGRN_CHUNKS_PER_STEP = 4
HGRN_PROJ_SUBTILE = 256
HGRN_PROJ_COLS = 256
N_SPLIT = 3

BF16 = jnp.bfloat16
F32 = jnp.float32


def _params(*sem):
    return pltpu.CompilerParams(dimension_semantics=sem, vmem_limit_bytes=VMEM_LIMIT_BYTES)


def _resident(shape):
    zeros = (0,) * len(shape)
    return pl.BlockSpec(shape, lambda *_: zeros, pipeline_mode=pl.Buffered(1))


def _row_spec(tm, width):
    return pl.BlockSpec((1, tm, width), lambda b, i: (b, i, 0))


def _block_major_spec(features, tm):
    return pl.BlockSpec((1, 1, features, tm), lambda b, i: (b, i, 0, 0))


def _batch_vec_spec(width):
    return pl.BlockSpec((1, 1, width), lambda b, i: (b, 0, 0))


def _dot(a, b):
    return jnp.dot(a, b, preferred_element_type=F32)


def _dot_nt(a, b):
    return lax.dot_general(a, b, (((1,), (1,)), ((), ())), preferred_element_type=F32)


def _dot_tn(a, b):
    return lax.dot_general(a, b, (((0,), (0,)), ((), ())), preferred_element_type=F32)


def _sigmoid(x):
    return 1.0 / (1.0 + jnp.exp(-x))


def _silu(x):
    return x * _sigmoid(x)


def _modulated_norm(x, norm_g, scale, shift):
    y = x * lax.rsqrt(jnp.mean(x * x, axis=-1, keepdims=True) + EPS)
    return (y * norm_g) * (1.0 + scale) + shift


def _split_bf16(x, n):
    pieces = []
    rest = x
    for _ in range(n):
        p = rest.astype(BF16)
        pieces.append(p)
        rest = rest - p.astype(F32)
    return pieces


def _ada_kernel(c_ref, w_ref, b_ref, o_ref):
    cond = _silu(c_ref[...])
    y = jnp.dot(cond, w_ref[0], preferred_element_type=F32,
                precision=lax.Precision.HIGHEST)
    o_ref[0] = y + b_ref[0]


def _ada_modulation(c, ada_w, ada_b):
    depth, d, n = ada_w.shape
    b = c.shape[0]
    bp = -(-b // SUBLANES) * SUBLANES
    tn = n // 9 if n % (9 * LANES) == 0 else n
    c_pad = jnp.zeros((bp, d), F32).at[:b].set(c)
    out = pl.pallas_call(
        _ada_kernel,
        grid=(depth, n // tn),
        in_specs=[
            pl.BlockSpec((bp, d), lambda i, j: (0, 0)),
            pl.BlockSpec((1, d, tn), lambda i, j: (i, 0, j)),
            pl.BlockSpec((1, 1, tn), lambda i, j: (i, 0, j)),
        ],
        out_specs=pl.BlockSpec((1, bp, tn), lambda i, j: (i, 0, j)),
        out_shape=jax.ShapeDtypeStruct((depth, bp, n), F32),
        compiler_params=_params("parallel", "parallel"),
        name="ada_modulation",
    )(c_pad, ada_w, ada_b.reshape(depth, 1, n))
    return out[:, :b]


def _ffn_kernel(*refs, tf, mixer, final_norm):
    if mixer is None:
        x_ref, shift_ref, scale_ref, gate_ref, ng_ref, wup_ref, wdn_ref, fg_ref = refs[:8]
    else:
        (x_ref, mix_ref, mgate_ref, wout_ref,
         shift_ref, scale_ref, gate_ref, ng_ref, wup_ref, wdn_ref, fg_ref) = refs[:11]
    o_ref, h_ref, acc_ref = refs[-3:]
    wup_ref, wdn_ref = wup_ref.at[0, 0], wdn_ref.at[0, 0]
    hidden = wdn_ref.shape[0]

    x = x_ref[0]
    if mixer == "rows":
        x = x + mgate_ref[0] * _dot(mix_ref[0], wout_ref[...])
    elif mixer == "features":
        parts = [_dot_tn(mix_ref[0, blk], wout_ref[...]) for blk in range(mix_ref.shape[1])]
        x = x + mgate_ref[0] * jnp.concatenate(parts, axis=0)
    o_ref[0] = x
    h_ref[...] = _modulated_norm(x, ng_ref[...], scale_ref[0], shift_ref[0]).astype(BF16)

    def up(f):
        return (_dot(h_ref[...], wup_ref[:, f:f + tf]),
                _dot(h_ref[...], wup_ref[:, hidden + f:hidden + f + tf]))

    ab = up(0)
    for f in range(0, hidden, tf):
        nxt = up(f + tf) if f + tf < hidden else None
        down = _dot((_silu(ab[0]) * ab[1]).astype(BF16), wdn_ref[f:f + tf, :])
        if f == 0:
            acc_ref[...] = down
        else:
            acc_ref[...] += down
        ab = nxt

    y = o_ref[0] + (0.5 * gate_ref[0]) * acc_ref[...]
    if final_norm:
        y = (y * lax.rsqrt(jnp.mean(y * y, axis=-1, keepdims=True) + EPS)) * fg_ref[...]
    o_ref[0] = y


def _ffn(x, shift, scale, gate, norm_g, w_up, w_down, which, final_g, final_norm, mixer=None):
    b, s, d = x.shape
    f = w_down.shape[2]
    stack_index = tuple(which) + (0, 0)
    stack_spec = lambda rows, cols: pl.BlockSpec((1, 1, rows, cols), lambda bb, i: stack_index,
                                                 pipeline_mode=pl.Buffered(1))
    tf = FFN_CHUNK
    assert f % tf == 0
    tm = ROW_TILE
    operands, specs, mode = [x], [_row_spec(tm, d)], None
    if mixer is not None:
        o, mgate, w_out = mixer
        if o.ndim == 4:
            mode, blk = "features", o.shape[3]
            o_spec = pl.BlockSpec((1, tm // blk, d, blk), lambda bb, i: (bb, i, 0, 0))
        else:
            mode, o_spec = "rows", _row_spec(tm, d)
        operands += [o, mgate, w_out.astype(BF16)]
        specs += [o_spec, _batch_vec_spec(d), _resident((d, d))]
    operands += [shift, scale, gate, norm_g.reshape(1, d), w_up, w_down, final_g.reshape(1, d)]
    specs += [_batch_vec_spec(d), _batch_vec_spec(d), _batch_vec_spec(d), _resident((1, d)),
              stack_spec(d, 2 * f), stack_spec(f, d), _resident((1, d))]
    kern = functools.partial(_ffn_kernel, tf=tf, mixer=mode, final_norm=final_norm)
    return pl.pallas_call(
        kern,
        grid=(b, s // tm),
        in_specs=specs,
        out_specs=_row_spec(tm, d),
        out_shape=jax.ShapeDtypeStruct((b, s, d), F32),
        scratch_shapes=[pltpu.VMEM((tm, d), BF16), pltpu.VMEM((tm, d), F32)],
        compiler_params=_params("parallel", "parallel"),
        name="ffn" + ("" if mode is None else "_" + mode) + ("_final" if final_norm else ""),
    )(*operands)


FOX_SLOT = 2 * FOX_HEAD_DIM
ONES_LANE = N_SPLIT * FOX_HEADS
FOX_VROWS = FOX_HEAD_DIM + 16
LOG2E = 1.4426950408889634


def _fox_proj_kernel(x_ref, shift_ref, scale_ref, ng_ref, wqk_ref, wvt_ref, vones_ref, wgt_ref,
                     wf_ref, bf_ref, tri_ref, place_ref, qa_ref, ka_ref, vt_ref, sgt_ref, carry_ref):
    @pl.when(pl.program_id(1) == 0)
    def _():
        carry_ref[...] = jnp.zeros_like(carry_ref)

    h = _modulated_norm(x_ref[0], ng_ref[...], scale_ref[0], shift_ref[0]).astype(BF16)
    z = _dot(h, wf_ref[...]) + bf_ref[...]
    logf = jnp.minimum(z, 0.0) - jnp.log(1.0 + jnp.exp(-jnp.abs(z)))
    pieces = jnp.concatenate(_split_bf16(logf, N_SPLIT), axis=1)
    part = _dot(tri_ref[...], pieces)
    cum = carry_ref[...] + ((part[:, :LANES] + part[:, LANES:2 * LANES]) + part[:, 2 * LANES:])
    carry_ref[...] = cum[-1:, :]

    lane = lax.broadcasted_iota(jnp.int32, cum.shape, 1)
    operand = jnp.where(lane == ONES_LANE, 1.0, 0.0).astype(F32)
    for j, p in enumerate(_split_bf16(cum * LOG2E, N_SPLIT)):
        pf = p.astype(F32)
        shifted = pf if j == 0 else pltpu.roll(pf, j * FOX_HEADS, axis=1)
        operand = jnp.where((lane >= j * FOX_HEADS) & (lane < (j + 1) * FOX_HEADS), shifted, operand)
    operand = operand.astype(BF16)
    d = wqk_ref.shape[0]
    slot_lane = lax.broadcasted_iota(jnp.int32, (operand.shape[0], FOX_SLOT), 1)
    width = FOX_HEADS * FOX_SLOT

    def project(w_cols, place_cols):
        packed = _dot(h, wqk_ref[:, w_cols:w_cols + d])
        return packed, _dot(operand, place_ref[:, place_cols:place_cols + width])

    def store_slots(out_ref, packed, decay):
        for hd in range(FOX_HEADS):
            pair = packed[:, (hd // 2) * FOX_SLOT:(hd // 2 + 1) * FOX_SLOT]
            if hd % 2:
                pair = pltpu.roll(pair, FOX_HEAD_DIM, axis=1)
            cols = slice(hd * FOX_SLOT, (hd + 1) * FOX_SLOT)
            out_ref[0, :, cols] = jnp.where(slot_lane < FOX_HEAD_DIM, pair, decay[:, cols]).astype(BF16)

    q_parts = project(0, 0)
    k_parts = project(d, width)
    store_slots(qa_ref, *q_parts)
    vt = _dot_nt(wvt_ref[...], h)
    store_slots(ka_ref, *k_parts)
    zg = _dot_nt(wgt_ref[...], h)
    vt_ref[0, 0] = (vt + vones_ref[...]).astype(BF16)
    sgt_ref[0, 0] = _sigmoid(zg).astype(BF16)


def _fox_placement():
    place = np.zeros((LANES, 2 * FOX_HEADS * FOX_SLOT), np.float32)
    k_off = FOX_HEADS * FOX_SLOT
    for h in range(FOX_HEADS):
        base = h * FOX_SLOT + FOX_HEAD_DIM
        for j in range(N_SPLIT):
            place[j * FOX_HEADS + h, base + j] = 1.0
            place[ONES_LANE, base + N_SPLIT + j] = 1.0
            place[ONES_LANE, k_off + base + j] = 1.0
            place[j * FOX_HEADS + h, k_off + base + N_SPLIT + j] = -1.0
    return jnp.asarray(place, BF16)


def _fox_proj(x, shift, scale, norm_g, w_in, b_f):
    b, s, d = x.shape
    hds, dh = FOX_HEADS, FOX_HEAD_DIM
    tm = ATT_BLOCK
    att_scale = dh ** -0.5

    wqk = jnp.concatenate([w_in[:, 0:d] * (att_scale * LOG2E), w_in[:, d:2 * d]], axis=1).astype(BF16)
    wv = w_in[:, 2 * d:3 * d].T.reshape(hds, dh, d)
    pad = FOX_VROWS - dh
    wvt = jnp.concatenate([wv, jnp.zeros((hds, pad, d), F32)], axis=1).reshape(hds * FOX_VROWS, d).astype(BF16)
    vones = jnp.concatenate([jnp.zeros((hds, dh, 1), F32), jnp.ones((hds, pad, 1), F32)],
                            axis=1).reshape(hds * FOX_VROWS, 1)
    wgt = w_in[:, 3 * d:4 * d].T.astype(BF16)
    wf = jnp.zeros((d, LANES), F32).at[:, :hds].set(w_in[:, 4 * d:]).astype(BF16)
    bf = jnp.zeros((1, LANES), F32).at[0, :hds].set(b_f)
    tri = jnp.tril(jnp.ones((tm, tm), F32)).astype(BF16)
    qw = hds * FOX_SLOT
    vrows = hds * FOX_VROWS
    return pl.pallas_call(
        _fox_proj_kernel,
        grid=(b, s // tm),
        in_specs=[
            _row_spec(tm, d), _batch_vec_spec(d), _batch_vec_spec(d), _resident((1, d)),
            _resident((d, 2 * d)), _resident((vrows, d)), _resident((vrows, 1)), _resident((d, d)),
            _resident((d, LANES)), _resident((1, LANES)), _resident((tm, tm)), _resident((LANES, 2 * qw)),
        ],
        out_specs=[_row_spec(tm, qw), _row_spec(tm, qw), _block_major_spec(vrows, tm),
                   _block_major_spec(d, tm)],
        out_shape=[
            jax.ShapeDtypeStruct((b, s, qw), BF16), jax.ShapeDtypeStruct((b, s, qw), BF16),
            jax.ShapeDtypeStruct((b, s // tm, vrows, tm), BF16), jax.ShapeDtypeStruct((b, s // tm, d, tm), BF16),
        ],
        scratch_shapes=[pltpu.VMEM((1, LANES), F32)],
        compiler_params=_params("parallel", "arbitrary"),
        name="fox_proj",
    )(x, shift, scale, norm_g.reshape(1, d), wqk, wvt, vones, wgt, wf, bf, tri, _fox_placement())


def _fox_attn_kernel(qa_ref, ka_ref, vt_ref, sgt_ref, o_ref,
                     acc_ref, m_ref, bias_ref, shift_ref, alpha_ref, s_ref, *, blk, nblk):
    dh = FOX_HEAD_DIM
    heads = range(2)
    half = blk // 2

    kp = lax.broadcasted_iota(jnp.int32, (blk, blk), 0)
    qp = lax.broadcasted_iota(jnp.int32, (blk, blk), 1)
    bias_ref[...] = jnp.where(kp <= qp, 0.0, -jnp.inf)
    acc_ref[...] = jnp.zeros_like(acc_ref)
    m_ref[...] = jnp.full_like(m_ref, -jnp.inf)

    def scores(ij, buf, diagonal, which):
        i, j = ij
        q = qa_ref[0, pl.ds(pl.multiple_of(i * blk, blk), blk), :]
        k = ka_ref[0, pl.ds(pl.multiple_of(j * blk, blk), blk), :]
        for hh in which:
            slot = slice(hh * FOX_SLOT, (hh + 1) * FOX_SLOT)
            m_old = m_ref[i, hh]
            if diagonal:
                s_lo = _dot_nt(k[:half, slot], q[:, slot]) + bias_ref[:half, :]
                s_hi = _dot_nt(k[half:, slot], q[half:, slot]) + bias_ref[half:, half:]
                top_lo = jnp.max(s_lo, axis=0, keepdims=True)
                top_hi = jnp.maximum(top_lo[:, half:], jnp.max(s_hi, axis=0, keepdims=True))
                m_new = jnp.maximum(m_old, jnp.concatenate([top_lo[:, :half], top_hi], axis=1))
                s_ref[buf, hh, :half, :] = s_lo
                s_ref[buf, hh, half:, half:] = s_hi
            else:
                s = _dot_nt(k[:, slot], q[:, slot])
                m_new = jnp.maximum(m_old, jnp.max(s, axis=0, keepdims=True))
                s_ref[buf, hh] = s
            m_ref[i, hh] = m_new
            shift_ref[buf, hh] = m_new
            alpha_ref[buf, hh] = jnp.exp2(m_old - m_new)

    def values(ij, buf, diagonal, which):
        i, j = ij
        vt = vt_ref[0, j]
        for hh in which:
            vth = vt[hh * FOX_VROWS:(hh + 1) * FOX_VROWS, :]
            shift = shift_ref[buf, hh]
            if diagonal:
                p_lo = jnp.exp2(s_ref[buf, hh, :half, :] - shift)
                p_hi = jnp.exp2(s_ref[buf, hh, half:, half:] - shift[:, half:])
                pv_lo = _dot(vth[:, :half], p_lo.astype(BF16))
                pv_hi = _dot(vth[:, half:], p_hi.astype(BF16))
                pv = jnp.concatenate([pv_lo[:, :half], pv_lo[:, half:] + pv_hi], axis=1)
                acc = alpha_ref[buf, hh] * acc_ref[i, hh] + pv
                rows = slice(hh * dh, (hh + 1) * dh)
                gated = acc[:dh] * (1.0 / acc[dh:dh + 1]) * sgt_ref[0, i, rows, :].astype(F32)
                o_ref[0, i, rows, :] = gated.astype(BF16)
            else:
                p = jnp.exp2(s_ref[buf, hh] - shift)
                acc_ref[i, hh] = alpha_ref[buf, hh] * acc_ref[i, hh] + _dot(vth, p.astype(BF16))

    def sweep(first, advance, n_pairs, diagonal, unroll):
        assert unroll % 2 == 0 and n_pairs > unroll

        def steps(carry, count):
            sc, va = carry
            for r in range(count):
                for hh in heads:
                    scores(sc, (r + 1) % 2, diagonal, (hh,))
                    values(va, r % 2, diagonal, (hh,))
                sc, va = advance(sc), advance(va)
            return sc, va

        scores(first, 0, diagonal, heads)
        carry = lax.fori_loop(0, (n_pairs - 1) // unroll, lambda t, c: steps(c, unroll),
                              (advance(first), first))
        tail = (n_pairs - 1) % unroll
        _, va = steps(carry, tail)
        values(va, tail % 2, diagonal, heads)

    def next_below_diagonal(ij):
        i, j = ij
        wrap = j + 1 == i
        return jnp.where(wrap, i + 1, i), jnp.where(wrap, 0, j + 1)

    zero, one = jnp.int32(0), jnp.int32(1)
    sweep((one, zero), next_below_diagonal, nblk * (nblk - 1) // 2, False, ATT_UNROLL)
    sweep((zero, zero), lambda ij: (ij[0] + 1, ij[1] + 1), nblk, True, ATT_UNROLL)


def _fox_attention(qa, ka, vt, sgt):
    b, nblk, d, blk = sgt.shape
    s = nblk * blk
    pairs = FOX_HEADS // 2
    pair_rows = 2 * FOX_HEAD_DIM
    kern = functools.partial(_fox_attn_kernel, blk=blk, nblk=nblk)
    return pl.pallas_call(
        kern,
        grid=(b, pairs),
        in_specs=[
            pl.BlockSpec((1, s, 2 * FOX_SLOT), lambda bb, p: (bb, 0, p)),
            pl.BlockSpec((1, s, 2 * FOX_SLOT), lambda bb, p: (bb, 0, p)),
            pl.BlockSpec((1, nblk, 2 * FOX_VROWS, blk), lambda bb, p: (bb, 0, p, 0)),
            pl.BlockSpec((1, nblk, pair_rows, blk), lambda bb, p: (bb, 0, p, 0)),
        ],
        out_specs=pl.BlockSpec((1, nblk, pair_rows, blk), lambda bb, p: (bb, 0, p, 0)),
        out_shape=jax.ShapeDtypeStruct((b, nblk, d, blk), BF16),
        scratch_shapes=[
            pltpu.VMEM((nblk, 2, FOX_VROWS, blk), F32),
            pltpu.VMEM((nblk, 2, 1, blk), F32),
            pltpu.VMEM((blk, blk), F32),
            pltpu.VMEM((2, 2, 1, blk), F32),
            pltpu.VMEM((2, 2, 1, blk), F32),
            pltpu.VMEM((2, 2, blk, blk), F32),
        ],
        compiler_params=_params("parallel", "parallel"),
        name="fox_attention",
    )(qa, ka, vt, sgt)


def _chunk_prefix_rows(x, chunk):
    n, lanes = x.shape
    row = lax.broadcasted_iota(jnp.int32, x.shape, 0) & (chunk - 1)
    shift = 1
    while shift < min(chunk, SUBLANES):
        x = x + jnp.where(row >= shift, pltpu.roll(x, shift, axis=0), 0.0)
        shift *= 2
    x = x.reshape(n // chunk, chunk, lanes)
    while shift < chunk:
        moved = jnp.concatenate([jnp.zeros((n // chunk, shift, lanes), F32), x[:, :chunk - shift]], axis=1)
        x = x + moved
        shift *= 2
    return x.reshape(n, lanes)


def _hgrn_proj_kernel(x_ref, shift_ref, scale_ref, ng_ref, w_ref, lbl_ref,
                      q_ref, kk_ref, g_ref, v_ref, sg_ref, *, d, layer, chunk):
    lg = lbl_ref[...]
    e = jnp.exp(lg - jnp.max(lg, axis=0, keepdims=True))
    sm = e / jnp.sum(e, axis=0, keepdims=True)
    lb = jnp.sum(sm[0:layer + 1], axis=0, keepdims=True) - sm[0:1]

    for r0 in range(0, x_ref.shape[1], HGRN_PROJ_SUBTILE):
        rows = slice(r0, r0 + HGRN_PROJ_SUBTILE)
        h = _modulated_norm(x_ref[0, rows, :], ng_ref[...], scale_ref[0], shift_ref[0]).astype(BF16)

        def project(task):
            section, c0 = task
            return _dot(h, w_ref[:, section * d + c0:section * d + c0 + HGRN_PROJ_COLS])

        def finish(task, z):
            section, c0 = task
            cols = slice(c0, c0 + HGRN_PROJ_COLS)
            if section == 0:
                q_ref[0, rows, cols] = z.astype(BF16)
            elif section == 1:
                f = lb[:, cols] + (1.0 - lb[:, cols]) * _sigmoid(z)
                kk_ref[0, rows, cols] = (1.0 - f).astype(BF16)
                g_ref[0, rows, cols] = _chunk_prefix_rows(jnp.log(f), chunk) * LOG2E
            elif section == 2:
                v_ref[0, rows, cols] = _silu(z).astype(BF16)
            else:
                sg_ref[0, rows, cols] = _silu(z).astype(BF16)

        tasks = [(section, c0) for section in range(4) for c0 in range(0, d, HGRN_PROJ_COLS)]
        z = project(tasks[0])
        for t, task in enumerate(tasks):
            nxt = project(tasks[t + 1]) if t + 1 < len(tasks) else None
            finish(task, z)
            z = nxt


def _hgrn_proj(x, shift, scale, norm_g, w_in, lb_logits, layer):
    b, s, d = x.shape
    depth = lb_logits.shape[0]
    tm = ROW_TILE // 2
    assert tm % HGRN_PROJ_SUBTILE == 0 and HGRN_PROJ_SUBTILE % HGRN_CHUNK == 0
    kern = functools.partial(_hgrn_proj_kernel, d=d, layer=layer, chunk=HGRN_CHUNK)
    bf = jax.ShapeDtypeStruct((b, s, d), BF16)
    return pl.pallas_call(
        kern,
        grid=(b, s // tm),
        in_specs=[
            _row_spec(tm, d), _batch_vec_spec(d), _batch_vec_spec(d), _resident((1, d)),
            _resident((d, 4 * d)), _resident((depth, d)),
        ],
        out_specs=[_row_spec(tm, d)] * 5,
        out_shape=[bf, bf, jax.ShapeDtypeStruct((b, s, d), F32), bf, bf],
        compiler_params=_params("parallel", "parallel"),
        name="hgrn_proj",
    )(x, shift, scale, norm_g.reshape(1, d), w_in.astype(BF16), lb_logits.astype(F32))


def _pivot_rows(g, half):
    n, lanes = g.shape
    if 2 * half >= SUBLANES:
        g3 = g.reshape(n // (2 * half), 2 * half, lanes)
        piv = jnp.broadcast_to(g3[:, half - 1:half, :], g3.shape)
        return piv.reshape(n, lanes)
    g3 = g.reshape(n // SUBLANES, SUBLANES, lanes)
    sub = lax.broadcasted_iota(jnp.int32, g3.shape, 1)
    out = None
    for grp in range(SUBLANES // (2 * half)):
        r = grp * 2 * half + half - 1
        cand = jnp.broadcast_to(g3[:, r:r + 1, :], g3.shape)
        out = cand if out is None else jnp.where(sub >= grp * 2 * half, cand, out)
    return out.reshape(n, lanes)


def _interleave_rows(lower, upper, half):
    n, lanes = lower.shape
    if half >= SUBLANES:
        lo3 = lower.reshape(n // (2 * half), 2 * half, lanes)
        up3 = upper.reshape(n // (2 * half), 2 * half, lanes)
        return jnp.concatenate([lo3[:, :half], up3[:, half:]], axis=1).reshape(n, lanes)
    row = lax.broadcasted_iota(jnp.int32, lower.shape, 0)
    return jnp.where((row & half) != 0, upper, lower)


def _hgrn_chunk_kernel(q_ref, kk_ref, g_ref, v_ref, sg_ref, gn_ref, o_ref, state_ref, level_ref,
                       *, chunk, heads):
    n_levels = chunk.bit_length() - 1

    @pl.when(pl.program_id(2) == 0)
    def _():
        state_ref[...] = jnp.zeros_like(state_ref)
        t_idx = lax.broadcasted_iota(jnp.int32, (chunk, chunk), 0)
        s_idx = lax.broadcasted_iota(jnp.int32, (chunk, chunk), 1)
        diff = t_idx ^ s_idx
        level = jnp.full((chunk, chunk), -1, jnp.int32)
        for lv in range(n_levels):
            level = level + (diff >= (1 << lv)).astype(jnp.int32)
        level_ref[...] = jnp.where(t_idx > s_idx, level, jnp.where(t_idx == s_idx, n_levels, -1))

    kdim = HGRN_EXPAND
    level = level_ref[...]
    at_level = [level == lv for lv in range(n_levels + 1)]
    row = lax.broadcasted_iota(jnp.int32, (chunk, kdim), 0)
    side = {1 << lv: jnp.where((row & (1 << lv)) != 0, 1.0, -1.0)
            for lv in range(n_levels) if (1 << lv) < SUBLANES}

    def level_exponent(g, half):
        if half in side:
            return (g - _pivot_rows(g, half)) * side[half]
        g3 = g.reshape(chunk // (2 * half), 2 * half, kdim)
        piv = g3[:, half - 1:half, :]
        return jnp.concatenate([piv - g3[:, :half], g3[:, half:] - piv], axis=1).reshape(chunk, kdim)

    def within_chunk(ci, hh):
        rows = slice(ci * chunk, (ci + 1) * chunk)
        cols = slice(hh * kdim, (hh + 1) * kdim)
        qb = q_ref[0, rows, cols]
        kb = kk_ref[0, rows, cols]
        q = qb.astype(F32)
        kk = kb.astype(F32)
        g = g_ref[0, rows, cols]
        a = jnp.where(at_level[n_levels], _dot_nt(qb, kb), 0.0)
        for lv in range(n_levels):
            half = 1 << lv
            y = (_interleave_rows(kk, q, half) * jnp.exp2(level_exponent(g, half))).astype(BF16)
            a = jnp.where(at_level[lv], _dot_nt(y, y), a)
        return a.astype(BF16)

    def across_chunks(ci, hh, a):
        rows = slice(ci * chunk, (ci + 1) * chunk)
        cols = slice(hh * kdim, (hh + 1) * kdim)
        q = q_ref[0, rows, cols].astype(F32)
        kk = kk_ref[0, rows, cols].astype(F32)
        v = v_ref[0, rows, cols]
        g = g_ref[0, rows, cols]
        g_last = g[chunk - 1:chunk, :]
        state_t = state_ref[hh]
        o = _dot_nt((q * jnp.exp2(g)).astype(BF16), state_t.astype(BF16)) + _dot(a, v)
        k_tail = (kk * jnp.exp2(g_last - g)).astype(BF16)
        state_ref[hh] = state_t * jnp.exp2(g_last) + _dot_tn(v, k_tail)

        o = o * lax.rsqrt(jnp.mean(o * o, axis=-1, keepdims=True) + EPS)
        o = (o * gn_ref[:, cols]) * sg_ref[0, rows, cols].astype(F32)
        o_ref[0, rows, cols] = o.astype(BF16)

    pending = None
    for ci in range(q_ref.shape[1] // chunk):
        for hh in range(heads):
            a = within_chunk(ci, hh)
            if pending is not None:
                across_chunks(*pending)
            pending = (ci, hh, a)
    across_chunks(*pending)


def _hgrn_chunks(q, kk, g, v, sg, g_norm):
    b, s, d = q.shape
    chunk, heads = HGRN_CHUNK, HGRN_HEADS_PER_STEP
    width = heads * HGRN_EXPAND
    step_rows = HGRN_CHUNKS_PER_STEP * chunk
    spec = pl.BlockSpec((1, step_rows, width), lambda bb, hp, c: (bb, c, hp))
    kern = functools.partial(_hgrn_chunk_kernel, chunk=chunk, heads=heads)
    return pl.pallas_call(
        kern,
        grid=(b, d // width, s // step_rows),
        in_specs=[spec] * 5 + [pl.BlockSpec((1, width), lambda bb, hp, c: (0, hp))],
        out_specs=spec,
        out_shape=jax.ShapeDtypeStruct((b, s, d), BF16),
        scratch_shapes=[pltpu.VMEM((heads, HGRN_EXPAND, HGRN_EXPAND), F32),
                        pltpu.VMEM((chunk, chunk), jnp.int32)],
        compiler_params=_params("parallel", "parallel", "arbitrary"),
        name="hgrn_chunks",
    )(q, kk, g, v, sg, g_norm.reshape(1, d).astype(F32))


def kernel(x, c, ada_w, ada_b, norm_g, ffn_w_up, ffn_w_down, fox_w_in, fox_b_f, fox_w_out,
           hgrn_w_in, hgrn_norm_g, hgrn_w_out, hgrn_lb_logits, final_norm_g):
    b, s, d = x.shape
    depth = ada_w.shape[0]
    mod = _ada_modulation(c, ada_w, ada_b).reshape(depth, b, N_SUB, 3, 1, d)
    w_up, w_down = ffn_w_up.astype(BF16), ffn_w_down.astype(BF16)
    for i in range(depth):
        shift = lambda sub: mod[i, :, sub, 0]
        scale = lambda sub: mod[i, :, sub, 1]
        gate = lambda sub: mod[i, :, sub, 2]
        x = _ffn(x, shift(0), scale(0), gate(0), norm_g[i, 0], w_up, w_down, (i, 0),
                 final_norm_g, False)
        j = i // 2
        if i % 2 == 0:
            qa, ka, vt, sgt = _fox_proj(x, shift(1), scale(1), norm_g[i, 1], fox_w_in[j], fox_b_f[j])
            mixer = (_fox_attention(qa, ka, vt, sgt), gate(1), fox_w_out[j])
        else:
            q, kk, g, v, sg = _hgrn_proj(x, shift(1), scale(1), norm_g[i, 1], hgrn_w_in[j],
                                         hgrn_lb_logits, i)
            mixer = (_hgrn_chunks(q, kk, g, v, sg, hgrn_norm_g[j]), gate(1), hgrn_w_out[j])
        x = _ffn(x, shift(2), scale(2), gate(2), norm_g[i, 2], w_up, w_down, (i, 1),
                 final_norm_g, i == depth - 1, mixer)
    return x
```
